```python
import math
import jax
import jax.numpy as jnp
from jax import lax
import numpy as np

D_MODEL = 1024
BATCH = 8
SEQ = 8192
DEPTH = 1

GRID_W = 64
N_META = 16
NA_HEADS = 8
NA_HEAD_DIM = 64
NA_WIDTH = NA_HEADS * NA_HEAD_DIM
WIN_ROWS = 8
WIN_COLS = 16
DN_HEADS = 4
DN_KEY_DIM = 128
DN_VAL_DIM = 128
DN_QK_WIDTH = DN_HEADS * DN_KEY_DIM
DN_V_WIDTH = DN_HEADS * DN_VAL_DIM
CONV_W = 5
CHUNK = 64
MIX_WIDTH = NA_WIDTH + DN_V_WIDTH
D_FF = ((8 * D_MODEL + 3 * 256 - 1) // (3 * 256)) * 256
IN_SPLITS = (NA_WIDTH, NA_WIDTH, NA_WIDTH,
             DN_QK_WIDTH, DN_QK_WIDTH, DN_V_WIDTH, DN_V_WIDTH,
             2 * DN_HEADS, 2 * DN_HEADS)
IN_WIDTH = sum(IN_SPLITS)
EPS = 1e-6
NEG = -1e30

kernel_name = 'hybrid_natten_gated_deltanet_block'


def rmsnorm(x, g):
    x32 = x.astype(jnp.float32)
    y = x32 * lax.rsqrt(jnp.mean(x32 * x32, axis=-1, keepdims=True) + EPS)
    return (y * g.astype(jnp.float32)).astype(x.dtype)


def l2norm(x):
    x32 = x.astype(jnp.float32)
    return x32 * lax.rsqrt(jnp.sum(x32 * x32, axis=-1, keepdims=True) + EPS)


def split_points():
    pts, acc = [], 0
    for s in IN_SPLITS[:-1]:
        acc += s
        pts.append(acc)
    return pts


def neighbourhood_attention(q, k, v, rel_bias, rows):
    B, L, _ = q.shape
    H, dh = NA_HEADS, NA_HEAD_DIM
    kr, kc = min(WIN_ROWS, rows), WIN_COLS
    q = (q * (dh ** -0.5)).reshape(B, L, H, dh)
    k = k.reshape(B, L, H, dh)
    v = v.reshape(B, L, H, dh)
    qm, km, vm = q[:, :N_META], k[:, :N_META], v[:, :N_META]
    qg = q[:, N_META:].reshape(B, rows, GRID_W, H, dh)
    kg = k[:, N_META:].reshape(B, rows, GRID_W, H, dh)
    vg = v[:, N_META:].reshape(B, rows, GRID_W, H, dh)

    r = np.arange(rows)
    c = np.arange(GRID_W)
    key_rows = np.clip(r - kr // 2, 0, rows - kr)[:, None] + np.arange(kr)[None, :]
    col_start = np.clip(c - kc // 2, 0, GRID_W - kc)
    col_in = (c[None, :] >= col_start[:, None]) & (c[None, :] < col_start[:, None] + kc)
    dr_idx = key_rows - r[:, None] + (WIN_ROWS - 1)
    dc_idx = np.clip(c[None, :] - c[:, None], 1 - WIN_COLS, WIN_COLS - 1) + (WIN_COLS - 1)

    k_win = kg[:, key_rows]
    v_win = vg[:, key_rows].reshape(B, rows, kr * GRID_W, H, dh)

    bias = rel_bias.astype(jnp.float32)[:, dr_idx][:, :, :, dc_idx]
    bias = jnp.transpose(bias, (0, 1, 3, 2, 4))
    s_grid = jnp.einsum('brqhd,brikhd->bhrqik', qg, k_win).astype(jnp.float32) + bias
    s_grid = jnp.where(col_in[:, None, :], s_grid, NEG)
    s_meta = jnp.einsum('brqhd,bmhd->bhrqm', qg, km).astype(jnp.float32)
    logits = jnp.concatenate([s_grid.reshape(B, H, rows, GRID_W, kr * GRID_W), s_meta], axis=-1)
    p = jax.nn.softmax(logits, axis=-1).astype(v.dtype)
    o_grid = (jnp.einsum('bhrqn,brnhd->brqhd', p[..., :kr * GRID_W], v_win)
              + jnp.einsum('bhrqm,bmhd->brqhd', p[..., kr * GRID_W:], vm))
    o_grid = o_grid.reshape(B, rows * GRID_W, H * dh)

    pm = jax.nn.softmax(jnp.einsum('bmhd,bnhd->bhmn', qm, km).astype(jnp.float32), axis=-1).astype(v.dtype)
    o_meta = jnp.einsum('bhmn,bnhd->bmhd', pm, vm).reshape(B, N_META, H * dh)
    return jnp.concatenate([o_meta, o_grid], axis=1)


def short_conv(x, w):
    C = x.shape[-1]
    y = lax.conv_general_dilated(x, w[:, None, :].astype(x.dtype), window_strides=(1,),
                                 padding=[(CONV_W // 2, CONV_W // 2)],
                                 dimension_numbers=('NWC', 'WIO', 'NWC'),
                                 feature_group_count=C)
    return jax.nn.silu(y)


def to_chunks(x):
    B, T, H = x.shape[:3]
    x = x.reshape((B, T // CHUNK, CHUNK, H) + x.shape[3:])
    return jnp.moveaxis(x, 3, 1)


def chunked_gated_delta(q, k, v, g, beta):
    B, T, H, dk = q.shape
    dv = v.shape[-1]
    qc, kc_, vc = to_chunks(q), to_chunks(k), to_chunks(v)
    gc = jnp.cumsum(to_chunks(g), axis=-1)
    bc = to_chunks(beta)[..., None]
    kb = kc_ * bc
    vb = vc * bc
    incl = np.tril(np.ones((CHUNK, CHUNK), dtype=bool))
    strict = np.tril(np.ones((CHUNK, CHUNK), dtype=bool), -1)
    diff = gc[..., :, None] - gc[..., None, :]
    decay_mat = jnp.where(incl, jnp.exp(jnp.where(incl, diff, 0.0)), 0.0)
    m = jnp.where(strict, jnp.einsum('bhnid,bhnjd->bhnij', kb, kc_) * decay_mat, 0.0)
    a_mat = m + jnp.eye(CHUNK, dtype=jnp.float32)
    rhs = jnp.concatenate([vb, kb * jnp.exp(gc)[..., None]], axis=-1)
    sol = lax.linalg.triangular_solve(a_mat, rhs, left_side=True, lower=True, unit_diagonal=True)
    u = sol[..., :dv]
    w = sol[..., dv:]
    qk = jnp.einsum('bhnid,bhnjd->bhnij', qc, kc_) * decay_mat
    q_dec = qc * jnp.exp(gc)[..., None]
    k_dec = kc_ * jnp.exp(gc[..., -1:] - gc)[..., None]
    chunk_decay = jnp.exp(gc[..., -1])

    def step(state, xs):
        qk_i, q_i, w_i, u_i, k_i, cd_i = xs
        v_new = u_i - jnp.einsum('bhck,bhkv->bhcv', w_i, state)
        o_i = jnp.einsum('bhck,bhkv->bhcv', q_i, state) + jnp.einsum('bhij,bhjv->bhiv', qk_i, v_new)
        state = state * cd_i[..., None, None] + jnp.einsum('bhck,bhcv->bhkv', k_i, v_new)
        return state, o_i

    xs = tuple(jnp.moveaxis(t, 2, 0) for t in (qk, q_dec, w, u, k_dec, chunk_decay))
    s0 = jnp.zeros((B, H, dk, dv), jnp.float32)
    _, o = lax.scan(step, s0, xs)
    return jnp.transpose(o, (1, 0, 2, 3, 4)).transpose(0, 1, 3, 2, 4).reshape(B, T, H, dv)


def gated_deltanet(q, k, v, z, b, a, conv_w, a_log, dt_bias, norm_g):
    B, L, _ = q.shape
    H = DN_HEADS
    qkv = short_conv(jnp.concatenate([q, k, v], axis=-1), conv_w)
    q, k, v = jnp.split(qkv, [DN_QK_WIDTH, 2 * DN_QK_WIDTH], axis=-1)
    q = l2norm(q.reshape(B, L, H, DN_KEY_DIM)) * (DN_KEY_DIM ** -0.5)
    k = l2norm(k.reshape(B, L, H, DN_KEY_DIM))
    v = v.reshape(B, L, H, DN_VAL_DIM).astype(jnp.float32)
    beta = jax.nn.sigmoid(b.astype(jnp.float32)).reshape(B, L, 2, H)
    g = -jnp.exp(a_log.astype(jnp.float32)) * jax.nn.softplus(
        a.astype(jnp.float32).reshape(B, L, 2, H) + dt_bias.astype(jnp.float32))
    pad = (-L) % CHUNK

    def padf(t):
        return jnp.pad(t, [(0, 0), (pad, 0)] + [(0, 0)] * (t.ndim - 2))

    qp, kp, vp, gp, bp = padf(q), padf(k), padf(v), padf(g), padf(beta)
    o_fwd = chunked_gated_delta(qp, kp, vp, gp[:, :, 0], bp[:, :, 0])
    o_bwd = jnp.flip(chunked_gated_delta(jnp.flip(qp, 1), jnp.flip(kp, 1), jnp.flip(vp, 1),
                                         jnp.flip(gp[:, :, 1], 1), jnp.flip(bp[:, :, 1], 1)), 1)
    o = (o_fwd + o_bwd)[:, pad:]
    o = rmsnorm(o, norm_g) * jax.nn.silu(z.astype(jnp.float32).reshape(B, L, H, DN_VAL_DIM))
    return o.reshape(B, L, DN_V_WIDTH).astype(z.dtype)


def setup_inputs(seed: int = 0) -> dict:
    key = jax.random.key(seed)
    ks = jax.random.split(key, 16)
    f32 = jnp.float32

    def nrm(k, shape, scale):
        return jax.random.normal(k, shape, f32) * scale

    x = nrm(ks[0], (BATCH, SEQ, D_MODEL), 1.0)
    meta_tokens = nrm(ks[1], (N_META, D_MODEL), 1.0)
    g_mix = 1.0 + nrm(ks[2], (DEPTH, D_MODEL), 0.01)
    w_in = nrm(ks[3], (DEPTH, D_MODEL, IN_WIDTH), D_MODEL ** -0.5)
    na_rel_bias = nrm(ks[4], (DEPTH, NA_HEADS, 2 * WIN_ROWS - 1, 2 * WIN_COLS - 1), 0.02)
    dn_conv_w = nrm(ks[5], (DEPTH, CONV_W, 2 * DN_QK_WIDTH + DN_V_WIDTH), CONV_W ** -0.5)
    dn_a_log = jnp.log(jax.random.uniform(ks[6], (DEPTH, 2, DN_HEADS), f32, 1.0, 16.0))
    dt = jnp.exp(jax.random.uniform(ks[7], (DEPTH, 2, DN_HEADS), f32, math.log(1e-3), math.log(0.1)))
    dn_dt_bias = dt + jnp.log(-jnp.expm1(-dt))
    dn_norm_g = 1.0 + nrm(ks[8], (DEPTH, DN_VAL_DIM), 0.01)
    w_out = nrm(ks[9], (DEPTH, MIX_WIDTH, D_MODEL), MIX_WIDTH ** -0.5)
    g_ffn = 1.0 + nrm(ks[10], (DEPTH, D_MODEL), 0.01)
    w_gate = nrm(ks[11], (DEPTH, D_MODEL, D_FF), D_MODEL ** -0.5)
    w_up = nrm(ks[12], (DEPTH, D_MODEL, D_FF), D_MODEL ** -0.5)
    w_down = nrm(ks[13], (DEPTH, D_FF, D_MODEL), D_FF ** -0.5)
    g_final = 1.0 + nrm(ks[14], (D_MODEL,), 0.01)
    return {'x': x, 'meta_tokens': meta_tokens, 'g_mix': g_mix, 'w_in': w_in,
            'na_rel_bias': na_rel_bias, 'dn_conv_w': dn_conv_w, 'dn_a_log': dn_a_log,
            'dn_dt_bias': dn_dt_bias, 'dn_norm_g': dn_norm_g, 'w_out': w_out, 'g_ffn': g_ffn,
            'w_gate': w_gate, 'w_up': w_up, 'w_down': w_down, 'g_final': g_final}


def reference(x, meta_tokens, g_mix, w_in, na_rel_bias, dn_conv_w, dn_a_log, dn_dt_bias,
              dn_norm_g, w_out, g_ffn, w_gate, w_up, w_down, g_final):
    B, S, D = x.shape
    rows = S // GRID_W
    meta = jnp.broadcast_to(meta_tokens[None].astype(x.dtype), (B, N_META, D))
    h = jnp.concatenate([meta, x], axis=1)
    pts = split_points()
    for l in range(DEPTH):
        u = rmsnorm(h, g_mix[l])
        proj = u @ w_in[l]
        na_q, na_k, na_v, dn_q, dn_k, dn_v, dn_z, dn_b, dn_a = jnp.split(proj, pts, axis=-1)
        y_na = neighbourhood_attention(na_q, na_k, na_v, na_rel_bias[l], rows)
        y_dn = gated_deltanet(dn_q, dn_k, dn_v, dn_z, dn_b, dn_a, dn_conv_w[l], dn_a_log[l],
                              dn_dt_bias[l], dn_norm_g[l])
        h = h + jnp.concatenate([y_na, y_dn], axis=-1) @ w_out[l]
        u = rmsnorm(h, g_ffn[l])
        h = h + (jax.nn.silu(u @ w_gate[l]) * (u @ w_up[l])) @ w_down[l]
    return rmsnorm(h, g_final)[:, N_META:]
```

```python
import functools
import math

import jax
import jax.numpy as jnp
import numpy as np
from jax import lax
from jax.experimental import pallas as pl
from jax.experimental.pallas import tpu as pltpu

F32 = jnp.float32
BF16 = jnp.bfloat16

GRID_W = 64
N_META = 16
NA_HEADS = 8
NA_HEAD_DIM = 64
NA_WIDTH = NA_HEADS * NA_HEAD_DIM
WIN_ROWS = 8
WIN_COLS = 16
DN_HEADS = 4
DN_DIM = 128
DN_WIDTH = DN_HEADS * DN_DIM
CONV_W = 5
CHUNK = 64
EPS = 1e-6
NEG = -1e30
LANES = 128

VMEM_LIMIT = 56 * 1024 * 1024


def _params(*sem):
    return pltpu.CompilerParams(dimension_semantics=sem, vmem_limit_bytes=VMEM_LIMIT)


def _const_spec(shape):
    nd = len(shape)
    return pl.BlockSpec(shape, lambda *_: (0,) * nd, pipeline_mode=pl.Buffered(1))


def _rms_scale(x, g):
    ms = jnp.mean(x * x, axis=-1, keepdims=True)
    return x * lax.rsqrt(ms + EPS) * g


def _inproj_kernel(x_ref, g_ref, wna_ref, wdn_ref, wba_ref, na_ref, dn_ref, ba_ref):
    u = _rms_scale(x_ref[...], g_ref[...]).astype(BF16)
    for c in range(wna_ref.shape[1] // 512):
        sl = slice(c * 512, (c + 1) * 512)
        na_ref[:, sl] = jnp.dot(u, wna_ref[:, sl], preferred_element_type=F32).astype(na_ref.dtype)
    for c in range(wdn_ref.shape[1] // 512):
        sl = slice(c * 512, (c + 1) * 512)
        dn_ref[:, sl] = jnp.dot(u, wdn_ref[:, sl], preferred_element_type=F32)
    ba_ref[...] = jnp.dot(u, wba_ref[...], preferred_element_type=F32)


def _inproj(x2d, g, wna, wdn, wba, tm):
    m, d = x2d.shape
    return pl.pallas_call(
        _inproj_kernel,
        grid=(m // tm,),
        in_specs=[
            pl.BlockSpec((tm, d), lambda i: (i, 0)),
            _const_spec(g.shape),
            _const_spec(wna.shape),
            _const_spec(wdn.shape),
            _const_spec(wba.shape),
        ],
        out_specs=[
            pl.BlockSpec((tm, wna.shape[1]), lambda i: (i, 0)),
            pl.BlockSpec((tm, wdn.shape[1]), lambda i: (i, 0)),
            pl.BlockSpec((tm, wba.shape[1]), lambda i: (i, 0)),
        ],
        out_shape=[
            jax.ShapeDtypeStruct((m, wna.shape[1]), BF16),
            jax.ShapeDtypeStruct((m, wdn.shape[1]), F32),
            jax.ShapeDtypeStruct((m, wba.shape[1]), F32),
        ],
        compiler_params=_params("parallel"),
        name="inproj",
    )(x2d, g, wna, wdn, wba)


def _tail_kernel(x_ref, yna_ref, of_ref, ob_ref, z_ref, gdn_ref, wona_ref, wodn_ref, gffn_ref,
                 wg_ref, wu_ref, wd_ref, gfin_ref, out_ref, *, ff_chunk):
    tm = x_ref.shape[0]
    o = of_ref[...] + ob_ref[...]
    z = z_ref[...]
    gdn = gdn_ref[...]
    ydn = []
    for h in range(DN_HEADS):
        sl = slice(h * DN_DIM, (h + 1) * DN_DIM)
        ydn.append(_rms_scale(o[:, sl], gdn) * (z[:, sl] * jax.nn.sigmoid(z[:, sl])))
    ydn = jnp.concatenate(ydn, axis=-1).astype(BF16)
    h1 = x_ref[...]
    h1 = h1 + jnp.dot(yna_ref[...], wona_ref[...], preferred_element_type=F32)
    h1 = h1 + jnp.dot(ydn, wodn_ref[...], preferred_element_type=F32)
    u = _rms_scale(h1, gffn_ref[...]).astype(BF16)
    out_ref[...] = h1
    for c in range(wg_ref.shape[1] // ff_chunk):
        sl = slice(c * ff_chunk, (c + 1) * ff_chunk)
        gate = jnp.dot(u, wg_ref[:, sl], preferred_element_type=F32)
        up = jnp.dot(u, wu_ref[:, sl], preferred_element_type=F32)
        act = (gate * jax.nn.sigmoid(gate) * up).astype(BF16)
        out_ref[...] += jnp.dot(act, wd_ref[sl, :], preferred_element_type=F32)
    out_ref[...] = _rms_scale(out_ref[...], gfin_ref[...])


def _tail(x2d, yna, o_f, o_b, z, gdn, wona, wodn, gffn, wg, wu, wd, gfin, tm, ff_chunk):
    m, d = x2d.shape
    row = lambda w: pl.BlockSpec((tm, w), lambda i: (i, 0))
    consts = (gdn, wona, wodn, gffn, wg, wu, wd, gfin)
    return pl.pallas_call(
        functools.partial(_tail_kernel, ff_chunk=ff_chunk),
        grid=(m // tm,),
        in_specs=[row(d), row(yna.shape[1]), row(o_f.shape[1]), row(o_b.shape[1]),
                  pl.BlockSpec((tm, DN_WIDTH), lambda i: (i, 3))]
        + [_const_spec(c.shape) for c in consts],
        out_specs=row(d),
        out_shape=jax.ShapeDtypeStruct((m, d), F32),
        compiler_params=_params("parallel"),
        name="tail",
    )(x2d, yna, o_f, o_b, z, *consts)


def _na_jnp(q, k, v, rel_bias, rows):
    B, L, _ = q.shape
    H, dh = NA_HEADS, NA_HEAD_DIM
    kr, kc = min(WIN_ROWS, rows), WIN_COLS
    q = q.reshape(B, L, H, dh)
    k = k.reshape(B, L, H, dh)
    v = v.reshape(B, L, H, dh)
    km, vm = k[:, :N_META], v[:, :N_META]
    qg = q[:, N_META:].reshape(B, rows, GRID_W, H, dh)
    kg = k[:, N_META:].reshape(B, rows, GRID_W, H, dh)
    vg = v[:, N_META:].reshape(B, rows, GRID_W, H, dh)
    r = np.arange(rows)
    c = np.arange(GRID_W)
    key_rows = np.clip(r - kr // 2, 0, rows - kr)[:, None] + np.arange(kr)[None, :]
    col_start = np.clip(c - kc // 2, 0, GRID_W - kc)
    col_in = (c[None, :] >= col_start[:, None]) & (c[None, :] < col_start[:, None] + kc)
    dr_idx = key_rows - r[:, None] + (WIN_ROWS - 1)
    dc_idx = np.clip(c[None, :] - c[:, None], 1 - WIN_COLS, WIN_COLS - 1) + (WIN_COLS - 1)
    k_win = kg[:, key_rows]
    v_win = vg[:, key_rows].reshape(B, rows, kr * GRID_W, H, dh)
    bias = rel_bias.astype(F32)[:, dr_idx][:, :, :, dc_idx]
    bias = jnp.transpose(bias, (0, 1, 3, 2, 4))
    s_grid = jnp.einsum('brqhd,brikhd->bhrqik', qg, k_win).astype(F32) + bias
    s_grid = jnp.where(col_in[:, None, :], s_grid, NEG)
    s_meta = jnp.einsum('brqhd,bmhd->bhrqm', qg, km).astype(F32)
    logits = jnp.concatenate([s_grid.reshape(B, H, rows, GRID_W, kr * GRID_W), s_meta], axis=-1)
    p = jax.nn.softmax(logits, axis=-1)
    o_grid = (jnp.einsum('bhrqn,brnhd->brqhd', p[..., :kr * GRID_W], v_win)
              + jnp.einsum('bhrqm,bmhd->brqhd', p[..., kr * GRID_W:], vm))
    return o_grid.reshape(B, rows * GRID_W, H * dh)


def _l2norm(x):
    return x * lax.rsqrt(jnp.sum(x * x, axis=-1, keepdims=True) + EPS)


def _to_chunks(x):
    B, T, H = x.shape[:3]
    x = x.reshape((B, T // CHUNK, CHUNK, H) + x.shape[3:])
    return jnp.moveaxis(x, 3, 1)


def _chunked_delta_jnp(q, k, v, g, beta):
    B, T, H, dk = q.shape
    dv = v.shape[-1]
    qc, kc_, vc = _to_chunks(q), _to_chunks(k), _to_chunks(v)
    gc = jnp.cumsum(_to_chunks(g), axis=-1)
    bc = _to_chunks(beta)[..., None]
    kb = kc_ * bc
    vb = vc * bc
    incl = np.tril(np.ones((CHUNK, CHUNK), dtype=bool))
    strict = np.tril(np.ones((CHUNK, CHUNK), dtype=bool), -1)
    diff = gc[..., :, None] - gc[..., None, :]
    decay_mat = jnp.where(incl, jnp.exp(jnp.where(incl, diff, 0.0)), 0.0)
    m = jnp.where(strict, jnp.einsum('bhnid,bhnjd->bhnij', kb, kc_) * decay_mat, 0.0)
    a_mat = m + jnp.eye(CHUNK, dtype=F32)
    rhs = jnp.concatenate([vb, kb * jnp.exp(gc)[..., None]], axis=-1)
    sol = lax.linalg.triangular_solve(a_mat, rhs, left_side=True, lower=True, unit_diagonal=True)
    u = sol[..., :dv]
    w = sol[..., dv:]
    qk = jnp.einsum('bhnid,bhnjd->bhnij', qc, kc_) * decay_mat
    q_dec = qc * jnp.exp(gc)[..., None]
    k_dec = kc_ * jnp.exp(gc[..., -1:] - gc)[..., None]
    chunk_decay = jnp.exp(gc[..., -1])

    def step(state, xs):
        qk_i, q_i, w_i, u_i, k_i, cd_i = xs
        v_new = u_i - jnp.einsum('bhck,bhkv->bhcv', w_i, state)
        o_i = jnp.einsum('bhck,bhkv->bhcv', q_i, state) + jnp.einsum('bhij,bhjv->bhiv', qk_i, v_new)
        state = state * cd_i[..., None, None] + jnp.einsum('bhck,bhcv->bhkv', k_i, v_new)
        return state, o_i

    xs = tuple(jnp.moveaxis(t, 2, 0) for t in (qk, q_dec, w, u, k_dec, chunk_decay))
    s0 = jnp.zeros((B, H, dk, dv), F32)
    _, o = lax.scan(step, s0, xs)
    return jnp.transpose(o, (1, 0, 2, 3, 4)).transpose(0, 1, 3, 2, 4).reshape(B, T, H, dv)


def _dn_jnp(q, k, v, b, a, conv_w, a_log, dt_bias):
    B, L, _ = q.shape
    H = DN_HEADS
    x = jnp.concatenate([q, k, v], axis=-1)
    C = x.shape[-1]
    y = lax.conv_general_dilated(x, conv_w[:, None, :], window_strides=(1,),
                                 padding=[(CONV_W // 2, CONV_W // 2)],
                                 dimension_numbers=('NWC', 'WIO', 'NWC'), feature_group_count=C)
    qkv = jax.nn.silu(y)
    q, k, v = jnp.split(qkv, [DN_WIDTH, 2 * DN_WIDTH], axis=-1)
    q = _l2norm(q.reshape(B, L, H, DN_DIM)) * (DN_DIM ** -0.5)
    k = _l2norm(k.reshape(B, L, H, DN_DIM))
    v = v.reshape(B, L, H, DN_DIM)
    beta = jax.nn.sigmoid(b).reshape(B, L, 2, H)
    g = -jnp.exp(a_log) * jax.nn.softplus(a.reshape(B, L, 2, H) + dt_bias)
    pad = (-L) % CHUNK

    def padf(t):
        return jnp.pad(t, [(0, 0), (pad, 0)] + [(0, 0)] * (t.ndim - 2))

    qp, kp, vp, gp, bp = padf(q), padf(k), padf(v), padf(g), padf(beta)
    o_fwd = _chunked_delta_jnp(qp, kp, vp, gp[:, :, 0], bp[:, :, 0])
    o_bwd = jnp.flip(_chunked_delta_jnp(jnp.flip(qp, 1), jnp.flip(kp, 1), jnp.flip(vp, 1),
                                        jnp.flip(gp[:, :, 1], 1), jnp.flip(bp[:, :, 1], 1)), 1)
    return o_fwd[:, pad:].reshape(B, L, DN_WIDTH), o_bwd[:, pad:].reshape(B, L, DN_WIDTH)


def kernel(x, meta_tokens, g_mix, w_in, na_rel_bias, dn_conv_w, dn_a_log, dn_dt_bias, dn_norm_g,
           w_out, g_ffn, w_gate, w_up, w_down, g_final):
    B, S, D = x.shape
    rows = S // GRID_W
    l = 0
    nqkv = 3 * NA_WIDTH
    ndn = 4 * DN_WIDTH
    w = w_in[l]
    wna = jnp.concatenate([w[:, :NA_WIDTH] * (NA_HEAD_DIM ** -0.5), w[:, NA_WIDTH:nqkv]], axis=1).astype(BF16)
    wdn = w[:, nqkv:nqkv + ndn].astype(BF16)
    wba = jnp.pad(w[:, nqkv + ndn:], ((0, 0), (0, LANES - 2 * 2 * DN_HEADS))).astype(BF16)
    gm = g_mix[l][None, :]

    x2d = x.reshape(B * S, D)
    na_x, dn_x, ba_x = _inproj(x2d, gm, wna, wdn, wba, tm=512)
    na_m, dn_m, ba_m = _inproj(meta_tokens, gm, wna, wdn, wba, tm=N_META)

    def full(xp, mp):
        c = xp.shape[-1]
        return jnp.concatenate([jnp.broadcast_to(mp[None], (B, N_META, c)), xp.reshape(B, S, c)], axis=1)

    na = full(na_x, na_m).astype(F32)
    dn = full(dn_x, dn_m)
    ba = full(ba_x, ba_m)
    y_na = _na_jnp(na[..., :NA_WIDTH], na[..., NA_WIDTH:2 * NA_WIDTH], na[..., 2 * NA_WIDTH:],
                   na_rel_bias[l], rows)
    o_f, o_b = _dn_jnp(dn[..., :DN_WIDTH], dn[..., DN_WIDTH:2 * DN_WIDTH], dn[..., 2 * DN_WIDTH:3 * DN_WIDTH],
                       ba[..., :2 * DN_HEADS], ba[..., 2 * DN_HEADS:4 * DN_HEADS],
                       dn_conv_w[l], dn_a_log[l], dn_dt_bias[l])
    o_f = o_f[:, N_META:].reshape(B * S, DN_WIDTH)
    o_b = o_b[:, N_META:].reshape(B * S, DN_WIDTH)
    z = dn_x

    wo = w_out[l].astype(BF16)
    out = _tail(x2d, y_na.reshape(B * S, NA_WIDTH).astype(BF16), o_f, o_b, z,
                dn_norm_g[l][None, :], wo[:NA_WIDTH], wo[NA_WIDTH:], g_ffn[l][None, :],
                w_gate[l].astype(BF16), w_up[l].astype(BF16), w_down[l].astype(BF16),
                g_final[None, :], tm=512, ff_chunk=256)
    return out.reshape(B, S, D)
```

```python
import functools

import jax
import jax.numpy as jnp
import numpy as np
from jax import lax
from jax.experimental import pallas as pl
from jax.experimental.pallas import tpu as pltpu

F32 = jnp.float32
BF16 = jnp.bfloat16

GRID_W = 64
N_META = 16
NA_HEADS = 8
NA_HEAD_DIM = 64
NA_WIDTH = NA_HEADS * NA_HEAD_DIM
WIN_ROWS = 8
WIN_COLS = 16
DN_HEADS = 4
DN_DIM = 128
DN_WIDTH = DN_HEADS * DN_DIM
CONV_W = 5
CHUNK = 64
EPS = 1e-6
NEG = -1e30
LANES = 128

VMEM_LIMIT = 56 * 1024 * 1024


def _params(*sem):
    return pltpu.CompilerParams(dimension_semantics=sem, vmem_limit_bytes=VMEM_LIMIT)


def _const_spec(shape):
    nd = len(shape)
    return pl.BlockSpec(shape, lambda *_: (0,) * nd, pipeline_mode=pl.Buffered(1))


def _rms_scale(x, g):
    ms = jnp.mean(x * x, axis=-1, keepdims=True)
    return x * lax.rsqrt(ms + EPS) * g


def _dot(a, b):
    return jnp.dot(a, b, preferred_element_type=F32)


def _inproj_kernel(x_ref, g_ref, wna_ref, wdn_ref, wba_ref, na_ref, dn_ref, ba_ref):
    u = _rms_scale(x_ref[...], g_ref[...]).astype(BF16)
    for c in range(wna_ref.shape[1] // 512):
        sl = slice(c * 512, (c + 1) * 512)
        na_ref[:, sl] = _dot(u, wna_ref[:, sl]).astype(na_ref.dtype)
    for c in range(wdn_ref.shape[1] // 512):
        sl = slice(c * 512, (c + 1) * 512)
        dn_ref[:, sl] = _dot(u, wdn_ref[:, sl])
    ba_ref[...] = _dot(u, wba_ref[...])


def _inproj(x2d, g, wna, wdn, wba, tm):
    m, d = x2d.shape
    return pl.pallas_call(
        _inproj_kernel,
        grid=(m // tm,),
        in_specs=[
            pl.BlockSpec((tm, d), lambda i: (i, 0)),
            _const_spec(g.shape),
            _const_spec(wna.shape),
            _const_spec(wdn.shape),
            _const_spec(wba.shape),
        ],
        out_specs=[
            pl.BlockSpec((tm, wna.shape[1]), lambda i: (i, 0)),
            pl.BlockSpec((tm, wdn.shape[1]), lambda i: (i, 0)),
            pl.BlockSpec((tm, wba.shape[1]), lambda i: (i, 0)),
        ],
        out_shape=[
            jax.ShapeDtypeStruct((m, wna.shape[1]), BF16),
            jax.ShapeDtypeStruct((m, wdn.shape[1]), F32),
            jax.ShapeDtypeStruct((m, wba.shape[1]), F32),
        ],
        compiler_params=_params("parallel"),
        name="inproj",
    )(x2d, g, wna, wdn, wba)


def _tail_kernel(x_ref, yna_ref, of_ref, ob_ref, z_ref, gdn_ref, wona_ref, wodn_ref, gffn_ref,
                 wg_ref, wu_ref, wd_ref, gfin_ref, out_ref, *, ff_chunk):
    o = of_ref[...] + ob_ref[...]
    z = z_ref[...]
    gdn = gdn_ref[...]
    ydn = []
    for h in range(DN_HEADS):
        sl = slice(h * DN_DIM, (h + 1) * DN_DIM)
        ydn.append(_rms_scale(o[:, sl], gdn) * (z[:, sl] * jax.nn.sigmoid(z[:, sl])))
    ydn = jnp.concatenate(ydn, axis=-1).astype(BF16)
    h1 = x_ref[...]
    h1 = h1 + _dot(yna_ref[...], wona_ref[...])
    h1 = h1 + _dot(ydn, wodn_ref[...])
    u = _rms_scale(h1, gffn_ref[...]).astype(BF16)
    out_ref[...] = h1
    for c in range(wg_ref.shape[1] // ff_chunk):
        sl = slice(c * ff_chunk, (c + 1) * ff_chunk)
        gate = _dot(u, wg_ref[:, sl])
        up = _dot(u, wu_ref[:, sl])
        act = (gate * jax.nn.sigmoid(gate) * up).astype(BF16)
        out_ref[...] += _dot(act, wd_ref[sl, :])
    out_ref[...] = _rms_scale(out_ref[...], gfin_ref[...])


def _tail(x2d, yna, o_f, o_b, z, gdn, wona, wodn, gffn, wg, wu, wd, gfin, tm, ff_chunk):
    m, d = x2d.shape
    row = lambda w: pl.BlockSpec((tm, w), lambda i: (i, 0))
    consts = (gdn, wona, wodn, gffn, wg, wu, wd, gfin)
    return pl.pallas_call(
        functools.partial(_tail_kernel, ff_chunk=ff_chunk),
        grid=(m // tm,),
        in_specs=[row(d), row(yna.shape[1]), row(o_f.shape[1]), row(o_b.shape[1]),
                  pl.BlockSpec((tm, DN_WIDTH), lambda i: (i, 3))]
        + [_const_spec(c.shape) for c in consts],
        out_specs=row(d),
        out_shape=jax.ShapeDtypeStruct((m, d), F32),
        compiler_params=_params("parallel"),
        name="tail",
    )(x2d, yna, o_f, o_b, z, *consts)


NA_ROWS_PER_STEP = 8
NA_PAIRS = NA_WIDTH // LANES


def _na_bias_table(rel_bias):
    c = np.arange(GRID_W)
    col_start = np.clip(c - WIN_COLS // 2, 0, GRID_W - WIN_COLS)
    col_in = (c[None, :] >= col_start[:, None]) & (c[None, :] < col_start[:, None] + WIN_COLS)
    dc_idx = np.clip(c[None, :] - c[:, None], 1 - WIN_COLS, WIN_COLS - 1) + (WIN_COLS - 1)
    dr_idx = np.arange(WIN_ROWS)[None, :] - np.arange(WIN_ROWS)[:, None] + (WIN_ROWS - 1)
    t = rel_bias.astype(F32)[:, dr_idx][:, :, :, dc_idx]
    t = jnp.where(col_in[None, None, None], t, NEG)
    t = jnp.transpose(t, (1, 0, 3, 2, 4))
    return t.reshape(WIN_ROWS, NA_HEADS, GRID_W, WIN_ROWS * GRID_W)


def _na_kernel(q_ref, kp_ref, kc_ref, kn_ref, vp_ref, vc_ref, vn_ref, meta_ref, tb_ref, out_ref,
               kbuf, vbuf, vmbuf, *, rows):
    i = pl.program_id(1)
    blk = NA_ROWS_PER_STEP * GRID_W
    nkey = WIN_ROWS * GRID_W
    ones_blk = jnp.ones((blk, LANES), BF16)
    for n, (kr, vr) in enumerate(((kp_ref, vp_ref), (kc_ref, vc_ref), (kn_ref, vn_ref))):
        kbuf[n * blk:(n + 1) * blk, :] = kr[0]
        for t in range(NA_PAIRS):
            vbuf[n * blk:(n + 1) * blk, 2 * t * LANES:(2 * t + 1) * LANES] = vr[0, :, t * LANES:(t + 1) * LANES]
            vbuf[n * blk:(n + 1) * blk, (2 * t + 1) * LANES:(2 * t + 2) * LANES] = ones_blk
    for t in range(NA_PAIRS):
        vmbuf[:, 2 * t * LANES:(2 * t + 1) * LANES] = meta_ref[:, 2 * NA_WIDTH + t * LANES:2 * NA_WIDTH + (t + 1) * LANES]
        vmbuf[:, (2 * t + 1) * LANES:(2 * t + 2) * LANES] = jnp.ones((N_META, LANES), BF16)
    nt = (((1,), (1,)), ((), ()))
    lane = lax.broadcasted_iota(jnp.int32, (GRID_W, LANES), 1)
    first = lane < NA_HEAD_DIM

    def row_body(j, carry):
        r = i * NA_ROWS_PER_STEP + j
        start = jnp.clip(r - WIN_ROWS // 2, 0, rows - WIN_ROWS)
        o = r - start
        off = pl.multiple_of((start - (i - 1) * NA_ROWS_PER_STEP) * GRID_W, GRID_W)
        qoff = pl.multiple_of(j * GRID_W, GRID_W)
        for t in range(NA_PAIRS):
            ls = slice(t * LANES, (t + 1) * LANES)
            qp = q_ref[0, pl.ds(qoff, GRID_W), ls]
            zero = jnp.zeros_like(qp)
            qs = jnp.concatenate([jnp.where(first, qp, zero), jnp.where(first, zero, qp)], axis=0)
            s = lax.dot_general(qs, kbuf[pl.ds(off, nkey), ls], nt, preferred_element_type=F32)
            s = s + tb_ref[o, t]
            sm = lax.dot_general(qs, meta_ref[:, NA_WIDTH + t * LANES:NA_WIDTH + (t + 1) * LANES], nt,
                                 preferred_element_type=F32)
            m = jnp.maximum(jnp.max(s, axis=-1, keepdims=True), jnp.max(sm, axis=-1, keepdims=True))
            p = jnp.exp(s - m).astype(BF16)
            pm = jnp.exp(sm - m).astype(BF16)
            acc = _dot(p, vbuf[pl.ds(off, nkey), 2 * t * LANES:(2 * t + 2) * LANES])
            acc = acc + _dot(pm, vmbuf[:, 2 * t * LANES:(2 * t + 2) * LANES])
            res = acc[:, :LANES] / acc[:, LANES:]
            out_ref[0, pl.ds(qoff, GRID_W), ls] = jnp.where(first, res[:GRID_W], res[GRID_W:]).astype(out_ref.dtype)
        return carry

    lax.fori_loop(0, NA_ROWS_PER_STEP, row_body, 0, unroll=4)


def _na(na_x, na_m, tb, B, S):
    rows = S // GRID_W
    nblk = rows // NA_ROWS_PER_STEP
    blk = NA_ROWS_PER_STEP * GRID_W
    na3 = na_x.reshape(B, S, 3 * NA_WIDTH)
    tb = tb.reshape(WIN_ROWS, NA_PAIRS, 2 * GRID_W, WIN_ROWS * GRID_W)

    def spec(col, shift):
        return pl.BlockSpec((1, blk, NA_WIDTH),
                            lambda b, i: (b, jnp.clip(i + shift, 0, nblk - 1), col))

    return pl.pallas_call(
        functools.partial(_na_kernel, rows=rows),
        grid=(B, nblk),
        in_specs=[spec(0, 0), spec(1, -1), spec(1, 0), spec(1, 1), spec(2, -1), spec(2, 0), spec(2, 1),
                  _const_spec(na_m.shape), _const_spec(tb.shape)],
        out_specs=pl.BlockSpec((1, blk, NA_WIDTH), lambda b, i: (b, i, 0)),
        out_shape=jax.ShapeDtypeStruct((B, S, NA_WIDTH), BF16),
        scratch_shapes=[pltpu.VMEM((3 * blk, NA_WIDTH), BF16), pltpu.VMEM((3 * blk, 2 * NA_WIDTH), BF16),
                        pltpu.VMEM((N_META, 2 * NA_WIDTH), BF16)],
        compiler_params=_params("parallel", "arbitrary"),
        name="nattn",
    )(na3, na3, na3, na3, na3, na3, na3, na_m, tb)


DN_TILE = 256
DN_SCAN_CHUNKS = 4
DN_SCAN_BATCH = 4
QUAD = DN_HEADS * CHUNK
GATE_B = 0
GATE_G = 2 * DN_HEADS
HALO = 8


def _dot_hilo(lhs_exact, x):
    hi = x.astype(BF16)
    lo = (x - hi.astype(F32)).astype(BF16)
    return _dot(lhs_exact, hi) + _dot(lhs_exact, lo)


def _hilo_dot(x, rhs_exact):
    hi = x.astype(BF16)
    lo = (x - hi.astype(F32)).astype(BF16)
    return _dot(hi, rhs_exact) + _dot(lo, rhs_exact)


def _conv_act(ext, cw, n):
    acc = None
    for j in range(CONV_W):
        lo = HALO + j - CONV_W // 2
        term = ext[lo:lo + n, :] * cw[j:j + 1, :]
        acc = term if acc is None else acc + term
    return acc * jax.nn.sigmoid(acc)


def _l2n(x):
    return x * lax.rsqrt(jnp.sum(x * x, axis=-1, keepdims=True) + EPS)


def _normalize_heads(y):
    qs, ks = [], []
    for h in range(DN_HEADS):
        qs.append(_l2n(y[:, h * DN_DIM:(h + 1) * DN_DIM]) * (DN_DIM ** -0.5))
        ks.append(_l2n(y[:, DN_WIDTH + h * DN_DIM:DN_WIDTH + (h + 1) * DN_DIM]))
    return jnp.concatenate(qs, axis=-1), jnp.concatenate(ks, axis=-1), y[:, 2 * DN_WIDTH:]


def _gate_tile(ba, gpar):
    lane = lax.broadcasted_iota(jnp.int32, ba.shape, 1)
    xg = ba + gpar[1:2, :]
    softplus = jnp.maximum(xg, 0.0) + jnp.log1p(jnp.exp(-jnp.abs(xg)))
    return jnp.where(lane < GATE_G, jax.nn.sigmoid(ba), gpar[0:1, :] * softplus)


def _blockdiag(x, nblk, col_blk):
    rows = x.shape[0]
    t = jnp.concatenate([x] * nblk, axis=0)
    rb = lax.broadcasted_iota(jnp.int32, t.shape, 0) // rows
    cb = (lax.broadcasted_iota(jnp.int32, t.shape, 1) // col_blk) % nblk
    return jnp.where(rb == cb, t, jnp.zeros_like(t))


def _tile_cumsums(g_tile, n):
    r = lax.broadcasted_iota(jnp.int32, (n, n), 0)
    c = lax.broadcasted_iota(jnp.int32, (n, n), 1)
    same = (r // CHUNK) == (c // CHUNK)
    allc = jnp.where(same, 1.0, 0.0)
    lower = jnp.where(c <= r, allc, 0.0).astype(BF16)
    upper = jnp.where(c >= r, allc, 0.0).astype(BF16)
    allc = allc.astype(BF16)
    lane = lax.broadcasted_iota(jnp.int32, g_tile.shape, 1)
    gc = jnp.where(lane < GATE_G + DN_HEADS, _dot_hilo(lower, g_tile), _dot_hilo(upper, g_tile))
    tot = _dot_hilo(allc, g_tile)
    return gc, tot


def _lockstep(gens):
    results = [None] * len(gens)
    alive = list(range(len(gens)))
    while alive:
        still = []
        for i in alive:
            try:
                next(gens[i])
                still.append(i)
            except StopIteration as stop:
                results[i] = stop.value
        alive = still
    return results


def _chunk_wy(q, k, v, gate, gc, tot, direction, shared):
    rr = lax.broadcasted_iota(jnp.int32, (CHUNK, QUAD), 0)
    cc = lax.broadcasted_iota(jnp.int32, (CHUNK, QUAD), 1) % CHUNK
    if direction == 0:
        incl, strict = rr >= cc, rr > cc
    else:
        incl, strict = rr <= cc, rr < cc
    eye4 = rr == cc
    col0 = direction * DN_HEADS
    qk_parts, kk_parts = [], []
    lane256 = lax.broadcasted_iota(jnp.int32, (CHUNK, 2 * DN_DIM), 1)
    for p in range(DN_HEADS // 2 if "qk4" not in shared else 0):
        sl = slice(2 * p * DN_DIM, (2 * p + 2) * DN_DIM)
        kp = k[:, sl].astype(BF16)
        zero = jnp.zeros_like(kp)
        rhs_t = jnp.concatenate([jnp.where(lane256 < DN_DIM, kp, zero), jnp.where(lane256 < DN_DIM, zero, kp)],
                                axis=0)
        lhs = jnp.concatenate([q[:, sl].astype(BF16), kp], axis=0)
        prod = lax.dot_general(lhs, rhs_t, (((1,), (1,)), ((), ())), preferred_element_type=F32)
        qk_parts.append(prod[:CHUNK])
        kk_parts.append(prod[CHUNK:])
    if qk_parts:
        shared["qk4"] = jnp.concatenate(qk_parts, axis=-1)
        shared["kk4"] = jnp.concatenate(kk_parts, axis=-1)
    qk4, kk4 = shared["qk4"], shared["kk4"]
    yield
    lane_sel = lax.broadcasted_iota(jnp.int32, (LANES, QUAD), 0)
    quad_head = lax.broadcasted_iota(jnp.int32, (LANES, QUAD), 1) // CHUNK
    sel_g = jnp.where(lane_sel == GATE_G + col0 + quad_head, 1.0, 0.0).astype(BF16)
    sel_b = jnp.where(lane_sel == GATE_B + col0 + quad_head, 1.0, 0.0).astype(BF16)
    colmat = _hilo_dot(gc, sel_g)
    beta4 = _hilo_dot(gate, sel_b)
    yield
    rowmat = _dot_hilo(jnp.ones((CHUNK, CHUNK), BF16), jnp.where(eye4, colmat, 0.0))
    yield
    dmat = jnp.where(incl, jnp.exp(jnp.where(incl, colmat - rowmat, 0.0)), 0.0)
    nmat = jnp.where(strict, -(beta4 * kk4 * dmat), 0.0)
    t4 = jnp.where(eye4, 1.0, 0.0) + nmat
    nb16 = nmat.astype(BF16)
    p4 = _dot(nb16, _blockdiag(nb16, DN_HEADS, CHUNK))
    yield
    for _ in range(int(np.log2(CHUNK)) - 2):
        both = _dot(jnp.concatenate([t4, p4], axis=0).astype(BF16), _blockdiag(p4.astype(BF16), DN_HEADS, CHUNK))
        t4 = t4 + both[:CHUNK]
        p4 = both[CHUNK:]
        yield
    t4 = t4 + _dot(t4.astype(BF16), _blockdiag(p4.astype(BF16), DN_HEADS, CHUNK))
    yield

    def wide_cols(tile, base):
        return jnp.concatenate(
            [jnp.broadcast_to(tile[:, base + col0 + h:base + col0 + h + 1], (CHUNK, DN_DIM))
             for h in range(DN_HEADS)], axis=-1)

    beta_w = wide_cols(gate, GATE_B)
    gc_w = wide_cols(gc, GATE_G)
    tot_w = wide_cols(tot, GATE_G)
    e_w = jnp.exp(gc_w)
    vb = v * beta_w
    kbg = k * beta_w * e_w
    q_dec = (q * e_w).astype(BF16)
    k_dec = k * jnp.exp(tot_w - gc_w)
    cd_row = jnp.exp(tot_w[0:1, :])
    rhs = jnp.concatenate([vb, kbg], axis=-1).astype(BF16)
    sol = _dot(t4.astype(BF16), _blockdiag(rhs, DN_HEADS, DN_DIM))
    u = sol[:, :DN_WIDTH]
    w = sol[:, DN_WIDTH:].astype(BF16)
    qk_out = (qk4 * dmat).astype(BF16)
    kd_t = []
    for p in range(DN_HEADS // 2):
        pair = jnp.concatenate([k_dec[:, 2 * p * DN_DIM:(2 * p + 1) * DN_DIM],
                                k_dec[:, (2 * p + 1) * DN_DIM:(2 * p + 2) * DN_DIM]], axis=0)
        kd_t.append(pair.T)
    kd_t = jnp.concatenate(kd_t, axis=-1).astype(BF16)
    return u, w, q_dec, qk_out, kd_t, cd_row


def _dn_prep_kernel(cur_ref, prev_ref, next_ref, ba_ref, meta_ref, cw_ref, gpar_ref,
                    u_ref, w_ref, qd_ref, qk_ref, kdt_ref, cd_ref,
                    ext_s, q_s, k_s, v_s, gate_s, gc_s, tot_s):
    t = pl.program_id(1)
    n = DN_TILE
    nqkv = 3 * DN_WIDTH
    prev = jnp.where(t == 0, meta_ref[N_META - HALO:N_META, :nqkv], prev_ref[0])
    nxt = jnp.where(t == pl.num_programs(1) - 1, jnp.zeros((HALO, nqkv), F32), next_ref[0])
    ext_s[0:HALO, :] = prev
    ext_s[HALO:HALO + n, :] = cur_ref[0]
    ext_s[HALO + n:2 * HALO + n, :] = nxt
    q, k, v = _normalize_heads(_conv_act(ext_s, cw_ref[...], n))
    q_s[...] = q
    k_s[...] = k
    v_s[...] = v
    gate = _gate_tile(ba_ref[0], gpar_ref[...])
    gate_s[...] = gate
    gc, tot = _tile_cumsums(gate, n)
    gc_s[...] = gc
    tot_s[...] = tot

    units = [(c, d) for c in range(n // CHUNK) for d in range(2)]
    gens = []
    shared = [{} for _ in range(n // CHUNK)]
    for c, d in units:
        rows = slice(c * CHUNK, (c + 1) * CHUNK)
        gens.append(_chunk_wy(q_s[rows, :], k_s[rows, :], v_s[rows, :],
                              gate_s[rows, :], gc_s[rows, :], tot_s[rows, :], d, shared[c]))
    for (c, d), (u, w, qd, qk, kdt, cd) in zip(units, _lockstep(gens)):
        rows = slice(c * CHUNK, (c + 1) * CHUNK)
        u_ref[d, 0, rows, :] = u
        w_ref[d, 0, rows, :] = w
        qd_ref[d, 0, rows, :] = qd
        qk_ref[d, 0, rows, :] = qk
        kdt_ref[d, 0, 2 * c * CHUNK:2 * (c + 1) * CHUNK, :] = kdt
        cd_ref[d, 0, c, 0, :, :] = cd


def _dn_prep(dn_x, ba_x, dn_m, cw, gpar, B, S):
    n = DN_TILE
    nch = n // CHUNK
    nqkv = 3 * DN_WIDTH
    dn3 = dn_x.reshape(B, S, dn_x.shape[-1])
    ba3 = ba_x.reshape(B, S, LANES)
    hb = n // HALO
    nhb = S // HALO
    outs = [
        ((2, B, S, DN_WIDTH), F32, (2, 1, n, DN_WIDTH)),
        ((2, B, S, DN_WIDTH), BF16, (2, 1, n, DN_WIDTH)),
        ((2, B, S, DN_WIDTH), BF16, (2, 1, n, DN_WIDTH)),
        ((2, B, S, QUAD), BF16, (2, 1, n, QUAD)),
        ((2, B, 2 * S, QUAD), BF16, (2, 1, 2 * n, QUAD)),
    ]
    out_specs = [pl.BlockSpec(blk, lambda b, t: (0, b, t, 0)) for _, _, blk in outs]
    out_shape = [jax.ShapeDtypeStruct(shp, dt) for shp, dt, _ in outs]
    out_specs.append(pl.BlockSpec((2, 1, nch, 1, 1, DN_WIDTH), lambda b, t: (0, b, t, 0, 0, 0)))
    out_shape.append(jax.ShapeDtypeStruct((2, B, S // CHUNK, 1, 1, DN_WIDTH), F32))
    return pl.pallas_call(
        _dn_prep_kernel,
        grid=(B, S // n),
        in_specs=[
            pl.BlockSpec((1, n, nqkv), lambda b, t: (b, t, 0)),
            pl.BlockSpec((1, HALO, nqkv), lambda b, t: (b, jnp.maximum(t * hb - 1, 0), 0)),
            pl.BlockSpec((1, HALO, nqkv), lambda b, t: (b, jnp.minimum((t + 1) * hb, nhb - 1), 0)),
            pl.BlockSpec((1, n, LANES), lambda b, t: (b, t, 0)),
            _const_spec(dn_m.shape), _const_spec(cw.shape), _const_spec(gpar.shape),
        ],
        out_specs=out_specs,
        out_shape=out_shape,
        scratch_shapes=[pltpu.VMEM((n + 2 * HALO, nqkv), F32)] + [pltpu.VMEM((n, DN_WIDTH), F32)] * 3
        + [pltpu.VMEM((n, LANES), F32)] * 3,
        compiler_params=_params("parallel", "parallel"),
        name="dn_prep",
    )(dn3, dn3, dn3, ba3, dn_m, cw, gpar)


def _state_step(s_ref, u, w, qd, qk, kdt, cd):
    s = s_ref[...]
    ws_parts, qs_parts = [], []
    for p in range(DN_HEADS // 2):
        sl = slice(2 * p * DN_DIM, (2 * p + 2) * DN_DIM)
        bd = _blockdiag(s[:, sl].astype(BF16), 2, DN_DIM)
        both = _dot(jnp.concatenate([w[:, sl], qd[:, sl]], axis=0), bd)
        ws_parts.append(both[:CHUNK])
        qs_parts.append(both[CHUNK:])
    yield
    v_new = u - jnp.concatenate(ws_parts, axis=-1)
    both = _dot(jnp.concatenate([qk, kdt], axis=0), _blockdiag(v_new.astype(BF16), DN_HEADS, DN_DIM))
    yield
    s_ref[...] = s * cd + both[CHUNK:]
    return jnp.concatenate(qs_parts, axis=-1) + both[:CHUNK]


def _dn_scan_kernel(sinit_ref, uf, wf, qdf, qkf, kdtf, cdf, ub, wb, qdb, qkb, kdtb, cdb,
                    of_ref, ob_ref, sf, sb, *, nch, nbatch):
    @pl.when(pl.program_id(1) == 0)
    def _():
        sf[...] = sinit_ref[...]
        sb[...] = jnp.zeros_like(sb)

    def body(c, carry):
        cb = nch - 1 - c
        rf = pl.ds(pl.multiple_of(c * CHUNK, CHUNK), CHUNK)
        rb = pl.ds(pl.multiple_of(cb * CHUNK, CHUNK), CHUNK)
        rf2 = pl.ds(pl.multiple_of(c * 2 * CHUNK, 2 * CHUNK), 2 * CHUNK)
        rb2 = pl.ds(pl.multiple_of(cb * 2 * CHUNK, 2 * CHUNK), 2 * CHUNK)
        gens = []
        for e in range(nbatch):
            gens.append(_state_step(sf.at[e], uf[0, e, rf, :], wf[0, e, rf, :], qdf[0, e, rf, :], qkf[0, e, rf, :],
                                    kdtf[0, e, rf2, :], cdf[0, e, c, 0]))
            gens.append(_state_step(sb.at[e], ub[0, e, rb, :], wb[0, e, rb, :], qdb[0, e, rb, :], qkb[0, e, rb, :],
                                    kdtb[0, e, rb2, :], cdb[0, e, cb, 0]))
        outs = _lockstep(gens)
        for e in range(nbatch):
            of_ref[e, rf, :] = outs[2 * e]
            ob_ref[e, rb, :] = outs[2 * e + 1]
        return carry

    lax.fori_loop(0, nch, body, 0)


def _dn_scan(sinit, u, w, qd, qk, kdt, cd, B, S):
    nch = DN_SCAN_CHUNKS
    n = nch * CHUNK
    nb = S // n
    nbatch = DN_SCAN_BATCH if B % DN_SCAN_BATCH == 0 else 1

    def specs(d):
        im = (lambda b, i: (d, b, i, 0)) if d == 0 else (lambda b, i: (d, b, nb - 1 - i, 0))
        im6 = (lambda b, i: (d, b, i, 0, 0, 0)) if d == 0 else (lambda b, i: (d, b, nb - 1 - i, 0, 0, 0))
        return [pl.BlockSpec((1, nbatch, n, DN_WIDTH), im), pl.BlockSpec((1, nbatch, n, DN_WIDTH), im),
                pl.BlockSpec((1, nbatch, n, DN_WIDTH), im), pl.BlockSpec((1, nbatch, n, QUAD), im),
                pl.BlockSpec((1, nbatch, 2 * n, QUAD), im), pl.BlockSpec((1, nbatch, nch, 1, 1, DN_WIDTH), im6)]

    return pl.pallas_call(
        functools.partial(_dn_scan_kernel, nch=nch, nbatch=nbatch),
        grid=(B // nbatch, nb),
        in_specs=[pl.BlockSpec((nbatch, DN_DIM, DN_WIDTH), lambda b, i: (b, 0, 0))] + specs(0) + specs(1),
        out_specs=[pl.BlockSpec((nbatch, n, DN_WIDTH), lambda b, i: (b, i, 0)),
                   pl.BlockSpec((nbatch, n, DN_WIDTH), lambda b, i: (b, nb - 1 - i, 0))],
        out_shape=[jax.ShapeDtypeStruct((B, S, DN_WIDTH), F32)] * 2,
        scratch_shapes=[pltpu.VMEM((nbatch, DN_DIM, DN_WIDTH), F32)] * 2,
        compiler_params=_params("parallel", "arbitrary"),
        name="dn_scan",
    )(sinit, u, w, qd, qk, kdt, cd, u, w, qd, qk, kdt, cd)


def _dn_meta_kernel(x0_ref, meta_ref, bam_ref, cw_ref, gpar_ref, sinit_ref, ext_s):
    nqkv = 3 * DN_WIDTH
    ext_s[0:HALO, :] = jnp.zeros((HALO, nqkv), F32)
    ext_s[HALO:HALO + N_META, :] = meta_ref[:, :nqkv]
    ext_s[HALO + N_META:, :] = x0_ref[0]
    q, k, v = _normalize_heads(_conv_act(ext_s, cw_ref[...], N_META))
    pad = CHUNK - N_META
    zw = jnp.zeros((pad, DN_WIDTH), F32)
    q, k, v = (jnp.concatenate([zw, a], axis=0) for a in (q, k, v))
    gate = jnp.concatenate([jnp.zeros((pad, LANES), F32), _gate_tile(bam_ref[...], gpar_ref[...])], axis=0)
    gc, tot = _tile_cumsums(gate, CHUNK)
    (u, _, _, _, kdt, _), = _lockstep([_chunk_wy(q, k, v, gate, gc, tot, 0, {})])
    sinit_ref[0] = _dot(kdt, _blockdiag(u.astype(BF16), DN_HEADS, DN_DIM))


def _dn_meta(dn_x, dn_m, ba_m, cw, gpar, B, S):
    nqkv = 3 * DN_WIDTH
    dn3 = dn_x.reshape(B, S, dn_x.shape[-1])
    return pl.pallas_call(
        _dn_meta_kernel,
        grid=(B,),
        in_specs=[pl.BlockSpec((1, HALO, nqkv), lambda b: (b, 0, 0)),
                  _const_spec(dn_m.shape), _const_spec(ba_m.shape), _const_spec(cw.shape), _const_spec(gpar.shape)],
        out_specs=pl.BlockSpec((1, DN_DIM, DN_WIDTH), lambda b: (b, 0, 0)),
        out_shape=jax.ShapeDtypeStruct((B, DN_DIM, DN_WIDTH), F32),
        scratch_shapes=[pltpu.VMEM((2 * HALO + N_META, nqkv), F32)],
        compiler_params=_params("parallel"),
        name="dn_meta",
    )(dn3, dn_m, ba_m, cw, gpar)


def _gate_params(a_log, dt_bias):
    neg_a = jnp.zeros((LANES,), F32).at[GATE_G:GATE_G + 2 * DN_HEADS].set(-jnp.exp(a_log.astype(F32)).reshape(-1))
    dtb = jnp.zeros((LANES,), F32).at[GATE_G:GATE_G + 2 * DN_HEADS].set(dt_bias.astype(F32).reshape(-1))
    return jnp.concatenate([neg_a[None], dtb[None], jnp.zeros((6, LANES), F32)], axis=0)


def _deltanet(dn_x, ba_x, dn_m, ba_m, conv_w, a_log, dt_bias, B, S):
    gpar = _gate_params(a_log, dt_bias)
    cw = jnp.pad(conv_w.astype(F32), ((0, 8 - CONV_W), (0, 0)))
    u, w, qd, qk, kdt, cd = _dn_prep(dn_x, ba_x, dn_m, cw, gpar, B, S)
    sinit = _dn_meta(dn_x, dn_m, ba_m, cw, gpar, B, S)
    return _dn_scan(sinit, u, w, qd, qk, kdt, cd, B, S)


def kernel(x, meta_tokens, g_mix, w_in, na_rel_bias, dn_conv_w, dn_a_log, dn_dt_bias, dn_norm_g,
           w_out, g_ffn, w_gate, w_up, w_down, g_final):
    B, S, D = x.shape
    l = 0
    nqkv = 3 * NA_WIDTH
    ndn = 4 * DN_WIDTH
    w = w_in[l]
    wna = jnp.concatenate([w[:, :NA_WIDTH] * (NA_HEAD_DIM ** -0.5), w[:, NA_WIDTH:nqkv]], axis=1).astype(BF16)
    wdn = w[:, nqkv:nqkv + ndn].astype(BF16)
    wba = jnp.pad(w[:, nqkv + ndn:], ((0, 0), (0, LANES - 4 * DN_HEADS))).astype(BF16)
    gm = g_mix[l][None, :]

    x2d = x.reshape(B * S, D)
    na_x, dn_x, ba_x = _inproj(x2d, gm, wna, wdn, wba, tm=512)
    na_m, dn_m, ba_m = _inproj(meta_tokens, gm, wna, wdn, wba, tm=N_META)

    y_na = _na(na_x, na_m, _na_bias_table(na_rel_bias[l]), B, S)
    o_f, o_b = _deltanet(dn_x, ba_x, dn_m, ba_m, dn_conv_w[l], dn_a_log[l], dn_dt_bias[l], B, S)

    wo = w_out[l].astype(BF16)
    out = _tail(x2d, y_na.reshape(B * S, NA_WIDTH), o_f.reshape(B * S, DN_WIDTH), o_b.reshape(B * S, DN_WIDTH),
                dn_x, dn_norm_g[l][None, :], wo[:NA_WIDTH], wo[NA_WIDTH:], g_ffn[l][None, :],
                w_gate[l].astype(BF16), w_up[l].astype(BF16), w_down[l].astype(BF16),
                g_final[None, :], tm=512, ff_chunk=256)
    return out.reshape(B, S, D)
```

```python
import functools

import jax
import jax.numpy as jnp
import numpy as np
from jax import lax
from jax.experimental import pallas as pl
from jax.experimental.pallas import tpu as pltpu

F32 = jnp.float32
BF16 = jnp.bfloat16

GRID_W = 64
N_META = 16
NA_HEADS = 8
NA_HEAD_DIM = 64
NA_WIDTH = NA_HEADS * NA_HEAD_DIM
WIN_ROWS = 8
WIN_COLS = 16
DN_HEADS = 4
DN_DIM = 128
DN_WIDTH = DN_HEADS * DN_DIM
CONV_W = 5
CHUNK = 64
EPS = 1e-6
NEG = -1e30
LANES = 128

VMEM_LIMIT = 56 * 1024 * 1024


def _params(*sem):
    return pltpu.CompilerParams(dimension_semantics=sem, vmem_limit_bytes=VMEM_LIMIT)


def _const_spec(shape):
    nd = len(shape)
    return pl.BlockSpec(shape, lambda *_: (0,) * nd, pipeline_mode=pl.Buffered(1))


def _rms_scale(x, g):
    ms = jnp.mean(x * x, axis=-1, keepdims=True)
    return x * lax.rsqrt(ms + EPS) * g


def _dot(a, b):
    return jnp.dot(a, b, preferred_element_type=F32)


def _inproj_kernel(x_ref, g_ref, wna_ref, wdn_ref, wba_ref, na_ref, dn_ref, ba_ref):
    u = _rms_scale(x_ref[...], g_ref[...]).astype(BF16)
    for c in range(wna_ref.shape[1] // 512):
        sl = slice(c * 512, (c + 1) * 512)
        na_ref[:, sl] = _dot(u, wna_ref[:, sl]).astype(na_ref.dtype)
    for c in range(wdn_ref.shape[1] // 512):
        sl = slice(c * 512, (c + 1) * 512)
        dn_ref[:, sl] = _dot(u, wdn_ref[:, sl])
    ba_ref[...] = _dot(u, wba_ref[...])


def _inproj(x2d, g, wna, wdn, wba, tm):
    m, d = x2d.shape
    return pl.pallas_call(
        _inproj_kernel,
        grid=(m // tm,),
        in_specs=[
            pl.BlockSpec((tm, d), lambda i: (i, 0)),
            _const_spec(g.shape),
            _const_spec(wna.shape),
            _const_spec(wdn.shape),
            _const_spec(wba.shape),
        ],
        out_specs=[
            pl.BlockSpec((tm, wna.shape[1]), lambda i: (i, 0)),
            pl.BlockSpec((tm, wdn.shape[1]), lambda i: (i, 0)),
            pl.BlockSpec((tm, wba.shape[1]), lambda i: (i, 0)),
        ],
        out_shape=[
            jax.ShapeDtypeStruct((m, wna.shape[1]), BF16),
            jax.ShapeDtypeStruct((m, wdn.shape[1]), F32),
            jax.ShapeDtypeStruct((m, wba.shape[1]), F32),
        ],
        compiler_params=_params("parallel"),
        name="inproj",
    )(x2d, g, wna, wdn, wba)


def _tail_kernel(x_ref, yna_ref, of_ref, ob_ref, z_ref, gdn_ref, wona_ref, wodn_ref, gffn_ref,
                 wg_ref, wu_ref, wd_ref, gfin_ref, out_ref, *, ff_chunk):
    o = of_ref[...] + ob_ref[...]
    z = z_ref[...]
    gdn = gdn_ref[...]
    ydn = []
    for h in range(DN_HEADS):
        sl = slice(h * DN_DIM, (h + 1) * DN_DIM)
        ydn.append(_rms_scale(o[:, sl], gdn) * (z[:, sl] * jax.nn.sigmoid(z[:, sl])))
    ydn = jnp.concatenate(ydn, axis=-1).astype(BF16)
    h1 = x_ref[...]
    h1 = h1 + _dot(yna_ref[...], wona_ref[...])
    h1 = h1 + _dot(ydn, wodn_ref[...])
    u = _rms_scale(h1, gffn_ref[...]).astype(BF16)
    out_ref[...] = h1
    for c in range(wg_ref.shape[1] // ff_chunk):
        sl = slice(c * ff_chunk, (c + 1) * ff_chunk)
        gate = _dot(u, wg_ref[:, sl])
        up = _dot(u, wu_ref[:, sl])
        act = (gate * jax.nn.sigmoid(gate) * up).astype(BF16)
        out_ref[...] += _dot(act, wd_ref[sl, :])
    out_ref[...] = _rms_scale(out_ref[...], gfin_ref[...])


def _tail(x2d, yna, o_f, o_b, z, gdn, wona, wodn, gffn, wg, wu, wd, gfin, tm, ff_chunk):
    m, d = x2d.shape
    row = lambda w: pl.BlockSpec((tm, w), lambda i: (i, 0))
    consts = (gdn, wona, wodn, gffn, wg, wu, wd, gfin)
    return pl.pallas_call(
        functools.partial(_tail_kernel, ff_chunk=ff_chunk),
        grid=(m // tm,),
        in_specs=[row(d), row(yna.shape[1]), row(o_f.shape[1]), row(o_b.shape[1]),
                  pl.BlockSpec((tm, DN_WIDTH), lambda i: (i, 3))]
        + [_const_spec(c.shape) for c in consts],
        out_specs=row(d),
        out_shape=jax.ShapeDtypeStruct((m, d), F32),
        compiler_params=_params("parallel"),
        name="tail",
    )(x2d, yna, o_f, o_b, z, *consts)


NA_ROWS_PER_STEP = 8
NA_PAIRS = NA_WIDTH // LANES


def _na_bias_table(rel_bias):
    c = np.arange(GRID_W)
    col_start = np.clip(c - WIN_COLS // 2, 0, GRID_W - WIN_COLS)
    col_in = (c[None, :] >= col_start[:, None]) & (c[None, :] < col_start[:, None] + WIN_COLS)
    dc_idx = np.clip(c[None, :] - c[:, None], 1 - WIN_COLS, WIN_COLS - 1) + (WIN_COLS - 1)
    onehot = (dc_idx[None] == np.arange(2 * WIN_COLS - 1)[:, None, None]) & col_in[None]
    onehot = jnp.asarray(onehot.reshape(2 * WIN_COLS - 1, GRID_W * GRID_W), F32)
    rb = rel_bias.astype(F32)
    rows_sel = jnp.stack([rb[:, WIN_ROWS - 1 - o:2 * WIN_ROWS - 1 - o, :] for o in range(WIN_ROWS)])
    t = jnp.dot(rows_sel.reshape(-1, 2 * WIN_COLS - 1), onehot, precision=lax.Precision.HIGHEST)
    t = t.reshape(WIN_ROWS, NA_HEADS, WIN_ROWS, GRID_W, GRID_W)
    t = jnp.where(col_in[None, None, None], t, NEG)
    t = jnp.transpose(t, (0, 1, 3, 2, 4))
    return t.reshape(WIN_ROWS, NA_HEADS, GRID_W, WIN_ROWS * GRID_W)


def _na_kernel(q_ref, kp_ref, kc_ref, kn_ref, vp_ref, vc_ref, vn_ref, meta_ref, tb_ref, out_ref,
               kbuf, vbuf, vmbuf, *, rows):
    i = pl.program_id(1)
    blk = NA_ROWS_PER_STEP * GRID_W
    nkey = WIN_ROWS * GRID_W
    ones_blk = jnp.ones((blk, LANES), BF16)
    for n, (kr, vr) in enumerate(((kp_ref, vp_ref), (kc_ref, vc_ref), (kn_ref, vn_ref))):
        kbuf[n * blk:(n + 1) * blk, :] = kr[0]
        for t in range(NA_PAIRS):
            vbuf[n * blk:(n + 1) * blk, 2 * t * LANES:(2 * t + 1) * LANES] = vr[0, :, t * LANES:(t + 1) * LANES]
            vbuf[n * blk:(n + 1) * blk, (2 * t + 1) * LANES:(2 * t + 2) * LANES] = ones_blk
    for t in range(NA_PAIRS):
        vmbuf[:, 2 * t * LANES:(2 * t + 1) * LANES] = meta_ref[:, 2 * NA_WIDTH + t * LANES:2 * NA_WIDTH + (t + 1) * LANES]
        vmbuf[:, (2 * t + 1) * LANES:(2 * t + 2) * LANES] = jnp.ones((N_META, LANES), BF16)
    nt = (((1,), (1,)), ((), ()))
    lane = lax.broadcasted_iota(jnp.int32, (GRID_W, LANES), 1)
    first = lane < NA_HEAD_DIM

    def row_body(j, carry):
        r = i * NA_ROWS_PER_STEP + j
        start = jnp.clip(r - WIN_ROWS // 2, 0, rows - WIN_ROWS)
        o = r - start
        off = pl.multiple_of((start - (i - 1) * NA_ROWS_PER_STEP) * GRID_W, GRID_W)
        qoff = pl.multiple_of(j * GRID_W, GRID_W)
        for t in range(NA_PAIRS):
            ls = slice(t * LANES, (t + 1) * LANES)
            qp = q_ref[0, pl.ds(qoff, GRID_W), ls]
            zero = jnp.zeros_like(qp)
            qs = jnp.concatenate([jnp.where(first, qp, zero), jnp.where(first, zero, qp)], axis=0)
            s = lax.dot_general(qs, kbuf[pl.ds(off, nkey), ls], nt, preferred_element_type=F32)
            s = s + tb_ref[o, t]
            sm = lax.dot_general(qs, meta_ref[:, NA_WIDTH + t * LANES:NA_WIDTH + (t + 1) * LANES], nt,
                                 preferred_element_type=F32)
            m = jnp.maximum(jnp.max(s, axis=-1, keepdims=True), jnp.max(sm, axis=-1, keepdims=True))
            p = jnp.exp(s - m).astype(BF16)
            pm = jnp.exp(sm - m).astype(BF16)
            acc = _dot(p, vbuf[pl.ds(off, nkey), 2 * t * LANES:(2 * t + 2) * LANES])
            acc = acc + _dot(pm, vmbuf[:, 2 * t * LANES:(2 * t + 2) * LANES])
            res = acc[:, :LANES] / acc[:, LANES:]
            out_ref[0, pl.ds(qoff, GRID_W), ls] = jnp.where(first, res[:GRID_W], res[GRID_W:]).astype(out_ref.dtype)
        return carry

    lax.fori_loop(0, NA_ROWS_PER_STEP, row_body, 0, unroll=True)


def _na(na_x, na_m, tb, B, S):
    rows = S // GRID_W
    nblk = rows // NA_ROWS_PER_STEP
    blk = NA_ROWS_PER_STEP * GRID_W
    na3 = na_x.reshape(B, S, 3 * NA_WIDTH)
    tb = tb.reshape(WIN_ROWS, NA_PAIRS, 2 * GRID_W, WIN_ROWS * GRID_W)

    def spec(col, shift):
        return pl.BlockSpec((1, blk, NA_WIDTH),
                            lambda b, i: (b, jnp.clip(i + shift, 0, nblk - 1), col))

    return pl.pallas_call(
        functools.partial(_na_kernel, rows=rows),
        grid=(B, nblk),
        in_specs=[spec(0, 0), spec(1, -1), spec(1, 0), spec(1, 1), spec(2, -1), spec(2, 0), spec(2, 1),
                  _const_spec(na_m.shape), _const_spec(tb.shape)],
        out_specs=pl.BlockSpec((1, blk, NA_WIDTH), lambda b, i: (b, i, 0)),
        out_shape=jax.ShapeDtypeStruct((B, S, NA_WIDTH), BF16),
        scratch_shapes=[pltpu.VMEM((3 * blk, NA_WIDTH), BF16), pltpu.VMEM((3 * blk, 2 * NA_WIDTH), BF16),
                        pltpu.VMEM((N_META, 2 * NA_WIDTH), BF16)],
        compiler_params=_params("parallel", "arbitrary"),
        name="nattn",
    )(na3, na3, na3, na3, na3, na3, na3, na_m, tb)


DN_TILE = 256
DN_SCAN_CHUNKS = 4
DN_SCAN_BATCH = 4
QUAD = DN_HEADS * CHUNK
GATE_B = 0
GATE_G = 2 * DN_HEADS
HALO = 8


def _dot_hilo(lhs_exact, x):
    hi = x.astype(BF16)
    lo = (x - hi.astype(F32)).astype(BF16)
    return _dot(lhs_exact, hi) + _dot(lhs_exact, lo)


def _hilo_dot(x, rhs_exact):
    hi = x.astype(BF16)
    lo = (x - hi.astype(F32)).astype(BF16)
    return _dot(hi, rhs_exact) + _dot(lo, rhs_exact)


def _conv_act(ext, cw, n):
    nv = n // HALO
    x3 = ext[...].reshape(nv + 2, HALO, ext.shape[1])
    sub = lax.broadcasted_iota(jnp.int32, (nv, HALO, ext.shape[1]), 1)
    acc = x3[1:nv + 1] * cw[CONV_W // 2]
    for j in range(CONV_W):
        d = j - CONV_W // 2
        if d == 0:
            continue
        r = pltpu.roll(x3, (-d) % HALO, 1)
        if d > 0:
            shifted = jnp.where(sub < HALO - d, r[1:nv + 1], r[2:nv + 2])
        else:
            shifted = jnp.where(sub >= -d, r[1:nv + 1], r[0:nv])
        acc = acc + shifted * cw[j]
    acc = acc.reshape(n, ext.shape[1])
    return acc * jax.nn.sigmoid(acc)


def _l2n(x):
    return x * lax.rsqrt(jnp.sum(x * x, axis=-1, keepdims=True) + EPS)


def _normalize_heads(y):
    qs, ks = [], []
    for h in range(DN_HEADS):
        qs.append(_l2n(y[:, h * DN_DIM:(h + 1) * DN_DIM]) * (DN_DIM ** -0.5))
        ks.append(_l2n(y[:, DN_WIDTH + h * DN_DIM:DN_WIDTH + (h + 1) * DN_DIM]))
    return jnp.concatenate(qs, axis=-1), jnp.concatenate(ks, axis=-1), y[:, 2 * DN_WIDTH:]


def _gate_tile(ba, gpar):
    lane = lax.broadcasted_iota(jnp.int32, ba.shape, 1)
    xg = ba + gpar[1:2, :]
    softplus = jnp.maximum(xg, 0.0) + jnp.log1p(jnp.exp(-jnp.abs(xg)))
    return jnp.where(lane < GATE_G, jax.nn.sigmoid(ba), gpar[0:1, :] * softplus)


def _blockdiag(x, nblk, col_blk):
    rows, width = x.shape
    if col_blk % LANES == 0:
        zero = jnp.zeros((rows, col_blk), x.dtype)
        return jnp.concatenate(
            [jnp.concatenate([x[:, j * col_blk:(j + 1) * col_blk] if j % nblk == u else zero
                              for j in range(width // col_blk)], axis=1) for u in range(nblk)], axis=0)
    t = jnp.concatenate([x] * nblk, axis=0)
    rb = lax.broadcasted_iota(jnp.int32, t.shape, 0) // rows
    cb = (lax.broadcasted_iota(jnp.int32, t.shape, 1) // col_blk) % nblk
    return jnp.where(rb == cb, t, jnp.zeros_like(t))


def _tile_cumsums(g_tile, n):
    r = lax.broadcasted_iota(jnp.int32, (n, n), 0)
    c = lax.broadcasted_iota(jnp.int32, (n, n), 1)
    same = (r // CHUNK) == (c // CHUNK)
    allc = jnp.where(same, 1.0, 0.0)
    lower = jnp.where(c <= r, allc, 0.0).astype(BF16)
    upper = jnp.where(c >= r, allc, 0.0).astype(BF16)
    allc = allc.astype(BF16)
    lane = lax.broadcasted_iota(jnp.int32, g_tile.shape, 1)
    gc = jnp.where(lane < GATE_G + DN_HEADS, _dot_hilo(lower, g_tile), _dot_hilo(upper, g_tile))
    tot = _dot_hilo(allc, g_tile)
    return gc, tot


def _lockstep(gens):
    results = [None] * len(gens)
    alive = list(range(len(gens)))
    while alive:
        still = []
        for i in alive:
            try:
                next(gens[i])
                still.append(i)
            except StopIteration as stop:
                results[i] = stop.value
        alive = still
    return results


def _decay_tiles(gate, n):
    gc, tot = _tile_cumsums(gate, n)
    return gc, jnp.exp(gc), jnp.exp(tot - gc), jnp.exp(tot)


def _chunk_wy(q, k, v, gate, gc, etile, ektile, cdtile, direction, shared):
    rr = lax.broadcasted_iota(jnp.int32, (CHUNK, QUAD), 0)
    cc = lax.broadcasted_iota(jnp.int32, (CHUNK, QUAD), 1) % CHUNK
    if direction == 0:
        incl, strict = rr >= cc, rr > cc
    else:
        incl, strict = rr <= cc, rr < cc
    eye4 = rr == cc
    col0 = direction * DN_HEADS
    qk_parts, kk_parts = [], []
    lane256 = lax.broadcasted_iota(jnp.int32, (CHUNK, 2 * DN_DIM), 1)
    for p in range(DN_HEADS // 2 if "qk4" not in shared else 0):
        sl = slice(2 * p * DN_DIM, (2 * p + 2) * DN_DIM)
        kp = k[:, sl].astype(BF16)
        zero = jnp.zeros_like(kp)
        rhs_t = jnp.concatenate([jnp.where(lane256 < DN_DIM, kp, zero), jnp.where(lane256 < DN_DIM, zero, kp)],
                                axis=0)
        lhs = jnp.concatenate([q[:, sl].astype(BF16), kp], axis=0)
        prod = lax.dot_general(lhs, rhs_t, (((1,), (1,)), ((), ())), preferred_element_type=F32)
        qk_parts.append(prod[:CHUNK])
        kk_parts.append(prod[CHUNK:])
    if qk_parts:
        shared["qk4"] = jnp.concatenate(qk_parts, axis=-1)
        shared["kk4"] = jnp.concatenate(kk_parts, axis=-1)
    qk4, kk4 = shared["qk4"], shared["kk4"]
    yield
    lane_sel = lax.broadcasted_iota(jnp.int32, (LANES, QUAD), 0)
    quad_head = lax.broadcasted_iota(jnp.int32, (LANES, QUAD), 1) // CHUNK
    sel_g = jnp.where(lane_sel == GATE_G + col0 + quad_head, 1.0, 0.0).astype(BF16)
    sel_b = jnp.where(lane_sel == GATE_B + col0 + quad_head, 1.0, 0.0).astype(BF16)
    colmat = _hilo_dot(gc, sel_g)
    beta4 = _hilo_dot(gate, sel_b)
    yield
    rowmat = _dot_hilo(jnp.ones((CHUNK, CHUNK), BF16), jnp.where(eye4, colmat, 0.0))
    yield
    dmat = jnp.where(incl, jnp.exp(jnp.where(incl, colmat - rowmat, 0.0)), 0.0)
    nmat = jnp.where(strict, -(beta4 * kk4 * dmat), 0.0)
    t4 = jnp.where(eye4, 1.0, 0.0) + nmat
    nb16 = nmat.astype(BF16)
    p4 = _dot(nb16, _blockdiag(nb16, DN_HEADS, CHUNK))
    yield
    for _ in range(int(np.log2(CHUNK)) - 2):
        both = _dot(jnp.concatenate([t4, p4], axis=0).astype(BF16), _blockdiag(p4.astype(BF16), DN_HEADS, CHUNK))
        t4 = t4 + both[:CHUNK]
        p4 = both[CHUNK:]
        yield
    t4 = t4 + _dot(t4.astype(BF16), _blockdiag(p4.astype(BF16), DN_HEADS, CHUNK))
    yield

    def wide_cols(tile, base):
        return jnp.concatenate(
            [jnp.broadcast_to(tile[:, base + col0 + h:base + col0 + h + 1], (CHUNK, DN_DIM))
             for h in range(DN_HEADS)], axis=-1)

    beta_w = wide_cols(gate, GATE_B)
    e_w = wide_cols(etile, GATE_G)
    vb = v * beta_w
    kbg = k * beta_w * e_w
    q_dec = (q * e_w).astype(BF16)
    k_dec = k * wide_cols(ektile, GATE_G)
    cd_row = wide_cols(cdtile, GATE_G)[0:1, :]
    rhs = jnp.concatenate([vb, kbg], axis=-1).astype(BF16)
    sol = _dot(t4.astype(BF16), _blockdiag(rhs, DN_HEADS, DN_DIM))
    u = sol[:, :DN_WIDTH]
    w = sol[:, DN_WIDTH:].astype(BF16)
    qk_out = (qk4 * dmat).astype(BF16)
    kd_t = []
    for p in range(DN_HEADS // 2):
        pair = jnp.concatenate([k_dec[:, 2 * p * DN_DIM:(2 * p + 1) * DN_DIM],
                                k_dec[:, (2 * p + 1) * DN_DIM:(2 * p + 2) * DN_DIM]], axis=0)
        kd_t.append(pair.T)
    kd_t = jnp.concatenate(kd_t, axis=-1).astype(BF16)
    return u, w, q_dec, qk_out, kd_t, cd_row


def _dn_prep_kernel(cur_ref, prev_ref, next_ref, ba_ref, meta_ref, cw_ref, gpar_ref,
                    u_ref, w_ref, qd_ref, qk_ref, kdt_ref, cd_ref,
                    ext_s, q_s, k_s, v_s, gate_s, gc_s, e_s, ek_s, cdt_s):
    t = pl.program_id(1)
    n = DN_TILE
    nqkv = 3 * DN_WIDTH
    prev = jnp.where(t == 0, meta_ref[N_META - HALO:N_META, :nqkv], prev_ref[0])
    nxt = jnp.where(t == pl.num_programs(1) - 1, jnp.zeros((HALO, nqkv), F32), next_ref[0])
    ext_s[0:HALO, :] = prev
    ext_s[HALO:HALO + n, :] = cur_ref[0]
    ext_s[HALO + n:2 * HALO + n, :] = nxt
    q, k, v = _normalize_heads(_conv_act(ext_s, cw_ref[...], n))
    q_s[...] = q
    k_s[...] = k
    v_s[...] = v
    gate = _gate_tile(ba_ref[0], gpar_ref[...])
    gate_s[...] = gate
    gc_s[...], e_s[...], ek_s[...], cdt_s[...] = _decay_tiles(gate, n)

    units = [(c, d) for c in range(n // CHUNK) for d in range(2)]
    gens = []
    shared = [{} for _ in range(n // CHUNK)]
    for c, d in units:
        rows = slice(c * CHUNK, (c + 1) * CHUNK)
        gens.append(_chunk_wy(q_s[rows, :], k_s[rows, :], v_s[rows, :],
                              gate_s[rows, :], gc_s[rows, :], e_s[rows, :], ek_s[rows, :], cdt_s[rows, :],
                              d, shared[c]))
    for (c, d), (u, w, qd, qk, kdt, cd) in zip(units, _lockstep(gens)):
        rows = slice(c * CHUNK, (c + 1) * CHUNK)
        u_ref[d, 0, rows, :] = u
        w_ref[d, 0, rows, :] = w
        qd_ref[d, 0, rows, :] = qd
        qk_ref[d, 0, rows, :] = qk
        kdt_ref[d, 0, 2 * c * CHUNK:2 * (c + 1) * CHUNK, :] = kdt
        cd_ref[d, 0, c, 0, :, :] = cd


def _dn_prep(dn_x, ba_x, dn_m, cw, gpar, B, S):
    n = DN_TILE
    nch = n // CHUNK
    nqkv = 3 * DN_WIDTH
    dn3 = dn_x.reshape(B, S, dn_x.shape[-1])
    ba3 = ba_x.reshape(B, S, LANES)
    hb = n // HALO
    nhb = S // HALO
    outs = [
        ((2, B, S, DN_WIDTH), F32, (2, 1, n, DN_WIDTH)),
        ((2, B, S, DN_WIDTH), BF16, (2, 1, n, DN_WIDTH)),
        ((2, B, S, DN_WIDTH), BF16, (2, 1, n, DN_WIDTH)),
        ((2, B, S, QUAD), BF16, (2, 1, n, QUAD)),
        ((2, B, 2 * S, QUAD), BF16, (2, 1, 2 * n, QUAD)),
    ]
    out_specs = [pl.BlockSpec(blk, lambda b, t: (0, b, t, 0)) for _, _, blk in outs]
    out_shape = [jax.ShapeDtypeStruct(shp, dt) for shp, dt, _ in outs]
    out_specs.append(pl.BlockSpec((2, 1, nch, 1, 1, DN_WIDTH), lambda b, t: (0, b, t, 0, 0, 0)))
    out_shape.append(jax.ShapeDtypeStruct((2, B, S // CHUNK, 1, 1, DN_WIDTH), F32))
    return pl.pallas_call(
        _dn_prep_kernel,
        grid=(B, S // n),
        in_specs=[
            pl.BlockSpec((1, n, nqkv), lambda b, t: (b, t, 0)),
            pl.BlockSpec((1, HALO, nqkv), lambda b, t: (b, jnp.maximum(t * hb - 1, 0), 0)),
            pl.BlockSpec((1, HALO, nqkv), lambda b, t: (b, jnp.minimum((t + 1) * hb, nhb - 1), 0)),
            pl.BlockSpec((1, n, LANES), lambda b, t: (b, t, 0)),
            _const_spec(dn_m.shape), _const_spec(cw.shape), _const_spec(gpar.shape),
        ],
        out_specs=out_specs,
        out_shape=out_shape,
        scratch_shapes=[pltpu.VMEM((n + 2 * HALO, nqkv), F32)] + [pltpu.VMEM((n, DN_WIDTH), F32)] * 3
        + [pltpu.VMEM((n, LANES), F32)] * 5,
        compiler_params=_params("parallel", "parallel"),
        name="dn_prep",
    )(dn3, dn3, dn3, ba3, dn_m, cw, gpar)


def _state_step(s_ref, u, w, qd, qk, kdt, cd):
    s = s_ref[...]
    ws_parts, qs_parts = [], []
    for p in range(DN_HEADS // 2):
        sl = slice(2 * p * DN_DIM, (2 * p + 2) * DN_DIM)
        bd = _blockdiag(s[:, sl].astype(BF16), 2, DN_DIM)
        both = _dot(jnp.concatenate([w[:, sl], qd[:, sl]], axis=0), bd)
        ws_parts.append(both[:CHUNK])
        qs_parts.append(both[CHUNK:])
    yield
    v_new = u - jnp.concatenate(ws_parts, axis=-1)
    both = _dot(jnp.concatenate([qk, kdt], axis=0), _blockdiag(v_new.astype(BF16), DN_HEADS, DN_DIM))
    yield
    s_ref[...] = s * cd + both[CHUNK:]
    return jnp.concatenate(qs_parts, axis=-1) + both[:CHUNK]


def _dn_scan_kernel(sinit_ref, uf, wf, qdf, qkf, kdtf, cdf, ub, wb, qdb, qkb, kdtb, cdb,
                    of_ref, ob_ref, sf, sb, *, nch, nbatch):
    @pl.when(pl.program_id(1) == 0)
    def _():
        sf[...] = sinit_ref[...]
        sb[...] = jnp.zeros_like(sb)

    def body(c, carry):
        cb = nch - 1 - c
        rf = pl.ds(pl.multiple_of(c * CHUNK, CHUNK), CHUNK)
        rb = pl.ds(pl.multiple_of(cb * CHUNK, CHUNK), CHUNK)
        rf2 = pl.ds(pl.multiple_of(c * 2 * CHUNK, 2 * CHUNK), 2 * CHUNK)
        rb2 = pl.ds(pl.multiple_of(cb * 2 * CHUNK, 2 * CHUNK), 2 * CHUNK)
        gens = []
        for e in range(nbatch):
            gens.append(_state_step(sf.at[e], uf[0, e, rf, :], wf[0, e, rf, :], qdf[0, e, rf, :], qkf[0, e, rf, :],
                                    kdtf[0, e, rf2, :], cdf[0, e, c, 0]))
            gens.append(_state_step(sb.at[e], ub[0, e, rb, :], wb[0, e, rb, :], qdb[0, e, rb, :], qkb[0, e, rb, :],
                                    kdtb[0, e, rb2, :], cdb[0, e, cb, 0]))
        outs = _lockstep(gens)
        for e in range(nbatch):
            of_ref[e, rf, :] = outs[2 * e]
            ob_ref[e, rb, :] = outs[2 * e + 1]
        return carry

    lax.fori_loop(0, nch, body, 0)


def _dn_scan(sinit, u, w, qd, qk, kdt, cd, B, S):
    nch = DN_SCAN_CHUNKS
    n = nch * CHUNK
    nb = S // n
    nbatch = DN_SCAN_BATCH if B % DN_SCAN_BATCH == 0 else 1

    def specs(d):
        im = (lambda b, i: (d, b, i, 0)) if d == 0 else (lambda b, i: (d, b, nb - 1 - i, 0))
        im6 = (lambda b, i: (d, b, i, 0, 0, 0)) if d == 0 else (lambda b, i: (d, b, nb - 1 - i, 0, 0, 0))
        return [pl.BlockSpec((1, nbatch, n, DN_WIDTH), im), pl.BlockSpec((1, nbatch, n, DN_WIDTH), im),
                pl.BlockSpec((1, nbatch, n, DN_WIDTH), im), pl.BlockSpec((1, nbatch, n, QUAD), im),
                pl.BlockSpec((1, nbatch, 2 * n, QUAD), im), pl.BlockSpec((1, nbatch, nch, 1, 1, DN_WIDTH), im6)]

    return pl.pallas_call(
        functools.partial(_dn_scan_kernel, nch=nch, nbatch=nbatch),
        grid=(B // nbatch, nb),
        in_specs=[pl.BlockSpec((nbatch, DN_DIM, DN_WIDTH), lambda b, i: (b, 0, 0))] + specs(0) + specs(1),
        out_specs=[pl.BlockSpec((nbatch, n, DN_WIDTH), lambda b, i: (b, i, 0)),
                   pl.BlockSpec((nbatch, n, DN_WIDTH), lambda b, i: (b, nb - 1 - i, 0))],
        out_shape=[jax.ShapeDtypeStruct((B, S, DN_WIDTH), F32)] * 2,
        scratch_shapes=[pltpu.VMEM((nbatch, DN_DIM, DN_WIDTH), F32)] * 2,
        compiler_params=_params("parallel", "arbitrary"),
        name="dn_scan",
    )(sinit, u, w, qd, qk, kdt, cd, u, w, qd, qk, kdt, cd)


def _dn_meta_kernel(x0_ref, meta_ref, bam_ref, cw_ref, gpar_ref, sinit_ref, ext_s):
    nqkv = 3 * DN_WIDTH
    ext_s[0:HALO, :] = jnp.zeros((HALO, nqkv), F32)
    ext_s[HALO:HALO + N_META, :] = meta_ref[:, :nqkv]
    ext_s[HALO + N_META:, :] = x0_ref[0]
    q, k, v = _normalize_heads(_conv_act(ext_s, cw_ref[...], N_META))
    pad = CHUNK - N_META
    zw = jnp.zeros((pad, DN_WIDTH), F32)
    q, k, v = (jnp.concatenate([zw, a], axis=0) for a in (q, k, v))
    gate = jnp.concatenate([jnp.zeros((pad, LANES), F32), _gate_tile(bam_ref[...], gpar_ref[...])], axis=0)
    gc, e, ek, cdt = _decay_tiles(gate, CHUNK)
    (u, _, _, _, kdt, _), = _lockstep([_chunk_wy(q, k, v, gate, gc, e, ek, cdt, 0, {})])
    sinit_ref[0] = _dot(kdt, _blockdiag(u.astype(BF16), DN_HEADS, DN_DIM))


def _dn_meta(dn_x, dn_m, ba_m, cw, gpar, B, S):
    nqkv = 3 * DN_WIDTH
    dn3 = dn_x.reshape(B, S, dn_x.shape[-1])
    return pl.pallas_call(
        _dn_meta_kernel,
        grid=(B,),
        in_specs=[pl.BlockSpec((1, HALO, nqkv), lambda b: (b, 0, 0)),
                  _const_spec(dn_m.shape), _const_spec(ba_m.shape), _const_spec(cw.shape), _const_spec(gpar.shape)],
        out_specs=pl.BlockSpec((1, DN_DIM, DN_WIDTH), lambda b: (b, 0, 0)),
        out_shape=jax.ShapeDtypeStruct((B, DN_DIM, DN_WIDTH), F32),
        scratch_shapes=[pltpu.VMEM((2 * HALO + N_META, nqkv), F32)],
        compiler_params=_params("parallel"),
        name="dn_meta",
    )(dn3, dn_m, ba_m, cw, gpar)


def _gate_params(a_log, dt_bias):
    neg_a = jnp.zeros((LANES,), F32).at[GATE_G:GATE_G + 2 * DN_HEADS].set(-jnp.exp(a_log.astype(F32)).reshape(-1))
    dtb = jnp.zeros((LANES,), F32).at[GATE_G:GATE_G + 2 * DN_HEADS].set(dt_bias.astype(F32).reshape(-1))
    return jnp.concatenate([neg_a[None], dtb[None], jnp.zeros((6, LANES), F32)], axis=0)


def _deltanet(dn_x, ba_x, dn_m, ba_m, conv_w, a_log, dt_bias, B, S):
    gpar = _gate_params(a_log, dt_bias)
    cw = jnp.pad(conv_w.astype(F32), ((0, 8 - CONV_W), (0, 0)))
    u, w, qd, qk, kdt, cd = _dn_prep(dn_x, ba_x, dn_m, cw, gpar, B, S)
    sinit = _dn_meta(dn_x, dn_m, ba_m, cw, gpar, B, S)
    return _dn_scan(sinit, u, w, qd, qk, kdt, cd, B, S)


INPROJ_TM = 1024


def kernel(x, meta_tokens, g_mix, w_in, na_rel_bias, dn_conv_w, dn_a_log, dn_dt_bias, dn_norm_g,
           w_out, g_ffn, w_gate, w_up, w_down, g_final):
    B, S, D = x.shape
    l = 0
    nqkv = 3 * NA_WIDTH
    ndn = 4 * DN_WIDTH
    w = w_in[l]
    wna = jnp.concatenate([w[:, :NA_WIDTH] * (NA_HEAD_DIM ** -0.5), w[:, NA_WIDTH:nqkv]], axis=1).astype(BF16)
    wdn = w[:, nqkv:nqkv + ndn].astype(BF16)
    wba = jnp.pad(w[:, nqkv + ndn:], ((0, 0), (0, LANES - 4 * DN_HEADS))).astype(BF16)
    gm = g_mix[l][None, :]

    x2d = x.reshape(B * S, D)
    na_x, dn_x, ba_x = _inproj(x2d, gm, wna, wdn, wba, tm=INPROJ_TM)
    na_m, dn_m, ba_m = _inproj(meta_tokens, gm, wna, wdn, wba, tm=N_META)

    y_na = _na(na_x, na_m, _na_bias_table(na_rel_bias[l]), B, S)
    o_f, o_b = _deltanet(dn_x, ba_x, dn_m, ba_m, dn_conv_w[l], dn_a_log[l], dn_dt_bias[l], B, S)

    wo = w_out[l].astype(BF16)
    out = _tail(x2d, y_na.reshape(B * S, NA_WIDTH), o_f.reshape(B * S, DN_WIDTH), o_b.reshape(B * S, DN_WIDTH),
                dn_x, dn_norm_g[l][None, :], wo[:NA_WIDTH], wo[NA_WIDTH:], g_ffn[l][None, :],
                w_gate[l].astype(BF16), w_up[l].astype(BF16), w_down[l].astype(BF16),
                g_final[None, :], tm=512, ff_chunk=256)
    return out.reshape(B, S, D)
```

```python
import functools

import jax
import jax.numpy as jnp
import numpy as np
from jax import lax
from jax.experimental import pallas as pl
from jax.experimental.pallas import tpu as pltpu

F32 = jnp.float32
BF16 = jnp.bfloat16

GRID_W = 64
N_META = 16
NA_HEADS = 8
NA_HEAD_DIM = 64
NA_WIDTH = NA_HEADS * NA_HEAD_DIM
WIN_ROWS = 8
WIN_COLS = 16
DN_HEADS = 4
DN_DIM = 128
DN_WIDTH = DN_HEADS * DN_DIM
CONV_W = 5
CHUNK = 64
EPS = 1e-6
NEG = -1e30
LANES = 128

VMEM_LIMIT = 56 * 1024 * 1024


def _params(*sem):
    return pltpu.CompilerParams(dimension_semantics=sem, vmem_limit_bytes=VMEM_LIMIT)


def _const_spec(shape):
    nd = len(shape)
    return pl.BlockSpec(shape, lambda *_: (0,) * nd, pipeline_mode=pl.Buffered(1))


def _rms_scale(x, g):
    ms = jnp.mean(x * x, axis=-1, keepdims=True)
    return x * lax.rsqrt(ms + EPS) * g


def _dot(a, b):
    return jnp.dot(a, b, preferred_element_type=F32)


PROJ_COLS = 512


def _inproj_kernel(x_ref, g_ref, wna_ref, wdn_ref, wba_ref, na_ref, dn_ref, ba_ref):
    u = _rms_scale(x_ref[...], g_ref[...]).astype(BF16)
    for c in range(wna_ref.shape[1] // PROJ_COLS):
        sl = slice(c * PROJ_COLS, (c + 1) * PROJ_COLS)
        na_ref[:, sl] = _dot(u, wna_ref[:, sl]).astype(na_ref.dtype)
    for c in range(wdn_ref.shape[1] // PROJ_COLS):
        sl = slice(c * PROJ_COLS, (c + 1) * PROJ_COLS)
        dn_ref[:, sl] = _dot(u, wdn_ref[:, sl])
    ba_ref[...] = _dot(u, wba_ref[...])


def _inproj(x2d, g, wna, wdn, wba, tm):
    m, d = x2d.shape
    return pl.pallas_call(
        _inproj_kernel,
        grid=(m // tm,),
        in_specs=[
            pl.BlockSpec((tm, d), lambda i: (i, 0)),
            _const_spec(g.shape),
            _const_spec(wna.shape),
            _const_spec(wdn.shape),
            _const_spec(wba.shape),
        ],
        out_specs=[
            pl.BlockSpec((tm, wna.shape[1]), lambda i: (i, 0)),
            pl.BlockSpec((tm, wdn.shape[1]), lambda i: (i, 0)),
            pl.BlockSpec((tm, wba.shape[1]), lambda i: (i, 0)),
        ],
        out_shape=[
            jax.ShapeDtypeStruct((m, wna.shape[1]), BF16),
            jax.ShapeDtypeStruct((m, wdn.shape[1]), F32),
            jax.ShapeDtypeStruct((m, wba.shape[1]), F32),
        ],
        compiler_params=_params("parallel"),
        name="inproj",
    )(x2d, g, wna, wdn, wba)


def _tail_kernel(x_ref, yna_ref, of_ref, ob_ref, z_ref, gdn_ref, wona_ref, wodn_ref, gffn_ref,
                 wg_ref, wu_ref, wd_ref, gfin_ref, out_ref, *, ff_chunk):
    o = of_ref[...] + ob_ref[...]
    z = z_ref[...]
    gdn = gdn_ref[...]
    ydn = []
    for h in range(DN_HEADS):
        sl = slice(h * DN_DIM, (h + 1) * DN_DIM)
        ydn.append(_rms_scale(o[:, sl], gdn) * (z[:, sl] * jax.nn.sigmoid(z[:, sl])))
    ydn = jnp.concatenate(ydn, axis=-1).astype(BF16)
    h1 = x_ref[...]
    h1 = h1 + _dot(yna_ref[...], wona_ref[...])
    h1 = h1 + _dot(ydn, wodn_ref[...])
    u = _rms_scale(h1, gffn_ref[...]).astype(BF16)
    out_ref[...] = h1
    for c in range(wg_ref.shape[1] // ff_chunk):
        sl = slice(c * ff_chunk, (c + 1) * ff_chunk)
        gate = _dot(u, wg_ref[:, sl])
        up = _dot(u, wu_ref[:, sl])
        act = (gate * jax.nn.sigmoid(gate) * up).astype(BF16)
        out_ref[...] += _dot(act, wd_ref[sl, :])
    out_ref[...] = _rms_scale(out_ref[...], gfin_ref[...])


def _tail(x2d, yna, o_f, o_b, z, gdn, wona, wodn, gffn, wg, wu, wd, gfin, tm, ff_chunk):
    m, d = x2d.shape
    assert m % tm == 0 and wg.shape[1] % ff_chunk == 0
    row = lambda w: pl.BlockSpec((tm, w), lambda i: (i, 0))
    consts = (gdn, wona, wodn, gffn, wg, wu, wd, gfin)
    return pl.pallas_call(
        functools.partial(_tail_kernel, ff_chunk=ff_chunk),
        grid=(m // tm,),
        in_specs=[row(d), row(yna.shape[1]), row(o_f.shape[1]), row(o_b.shape[1]),
                  pl.BlockSpec((tm, DN_WIDTH), lambda i: (i, 3))]
        + [_const_spec(c.shape) for c in consts],
        out_specs=row(d),
        out_shape=jax.ShapeDtypeStruct((m, d), F32),
        compiler_params=_params("parallel"),
        name="tail",
    )(x2d, yna, o_f, o_b, z, *consts)


NA_ROWS_PER_STEP = 16
NA_HALO_ROWS = 8
NA_PAIRS = NA_WIDTH // LANES


def _na_bias_table(rel_bias):
    c = np.arange(GRID_W)
    col_start = np.clip(c - WIN_COLS // 2, 0, GRID_W - WIN_COLS)
    col_in = (c[None, :] >= col_start[:, None]) & (c[None, :] < col_start[:, None] + WIN_COLS)
    dc_idx = np.clip(c[None, :] - c[:, None], 1 - WIN_COLS, WIN_COLS - 1) + (WIN_COLS - 1)
    onehot = (dc_idx[None] == np.arange(2 * WIN_COLS - 1)[:, None, None]) & col_in[None]
    onehot = jnp.asarray(onehot.reshape(2 * WIN_COLS - 1, GRID_W * GRID_W), F32)
    rb = rel_bias.astype(F32)
    rows_sel = jnp.stack([rb[:, WIN_ROWS - 1 - o:2 * WIN_ROWS - 1 - o, :] for o in range(WIN_ROWS)])
    t = jnp.dot(rows_sel.reshape(-1, 2 * WIN_COLS - 1), onehot, precision=lax.Precision.HIGHEST)
    t = t.reshape(WIN_ROWS, NA_HEADS, WIN_ROWS, GRID_W, GRID_W)
    t = jnp.where(col_in[None, None, None], t, NEG)
    t = jnp.transpose(t, (0, 1, 3, 2, 4))
    return t.reshape(WIN_ROWS, NA_HEADS, GRID_W, WIN_ROWS * GRID_W)


def _na_kernel(q_ref, kp_ref, kc_ref, kn_ref, vp_ref, vc_ref, vn_ref, meta_ref, tb_ref, out_ref,
               kbuf, vbuf, vmbuf, *, rows):
    i = pl.program_id(1)
    blk = NA_ROWS_PER_STEP * GRID_W
    hblk = NA_HALO_ROWS * GRID_W
    nkey = WIN_ROWS * GRID_W
    lo = 0
    for kr, vr, size in ((kp_ref, vp_ref, hblk), (kc_ref, vc_ref, blk), (kn_ref, vn_ref, hblk)):
        kbuf[lo:lo + size, :] = kr[0]
        for t in range(NA_PAIRS):
            vbuf[lo:lo + size, 2 * t * LANES:(2 * t + 1) * LANES] = vr[0, :, t * LANES:(t + 1) * LANES]
            vbuf[lo:lo + size, (2 * t + 1) * LANES:(2 * t + 2) * LANES] = jnp.ones((size, LANES), BF16)
        lo += size
    for t in range(NA_PAIRS):
        vmbuf[:, 2 * t * LANES:(2 * t + 1) * LANES] = meta_ref[:, 2 * NA_WIDTH + t * LANES:2 * NA_WIDTH + (t + 1) * LANES]
        vmbuf[:, (2 * t + 1) * LANES:(2 * t + 2) * LANES] = jnp.ones((N_META, LANES), BF16)
    nt = (((1,), (1,)), ((), ()))
    lane = lax.broadcasted_iota(jnp.int32, (GRID_W, LANES), 1)
    first = lane < NA_HEAD_DIM

    def row_body(j, carry):
        r = i * NA_ROWS_PER_STEP + j
        start = jnp.clip(r - WIN_ROWS // 2, 0, rows - WIN_ROWS)
        o = r - start
        off = pl.multiple_of((start - (i * NA_ROWS_PER_STEP - NA_HALO_ROWS)) * GRID_W, GRID_W)
        qoff = pl.multiple_of(j * GRID_W, GRID_W)
        for t in range(NA_PAIRS):
            ls = slice(t * LANES, (t + 1) * LANES)
            qp = q_ref[0, pl.ds(qoff, GRID_W), ls]
            zero = jnp.zeros_like(qp)
            qs = jnp.concatenate([jnp.where(first, qp, zero), jnp.where(first, zero, qp)], axis=0)
            s = lax.dot_general(qs, kbuf[pl.ds(off, nkey), ls], nt, preferred_element_type=F32)
            s = s + tb_ref[o, t]
            sm = lax.dot_general(qs, meta_ref[:, NA_WIDTH + t * LANES:NA_WIDTH + (t + 1) * LANES], nt,
                                 preferred_element_type=F32)
            m = jnp.maximum(jnp.max(s, axis=-1, keepdims=True), jnp.max(sm, axis=-1, keepdims=True))
            p = jnp.exp(s - m).astype(BF16)
            pm = jnp.exp(sm - m).astype(BF16)
            acc = _dot(p, vbuf[pl.ds(off, nkey), 2 * t * LANES:(2 * t + 2) * LANES])
            acc = acc + _dot(pm, vmbuf[:, 2 * t * LANES:(2 * t + 2) * LANES])
            res = acc[:, :LANES] / acc[:, LANES:]
            out_ref[0, pl.ds(qoff, GRID_W), ls] = jnp.where(first, res[:GRID_W], res[GRID_W:]).astype(out_ref.dtype)
        return carry

    lax.fori_loop(0, NA_ROWS_PER_STEP, row_body, 0, unroll=True)


def _na(na_x, na_m, tb, B, S):
    rows = S // GRID_W
    nblk = rows // NA_ROWS_PER_STEP
    blk = NA_ROWS_PER_STEP * GRID_W
    na3 = na_x.reshape(B, S, 3 * NA_WIDTH)
    tb = tb.reshape(WIN_ROWS, NA_PAIRS, 2 * GRID_W, WIN_ROWS * GRID_W)

    hblk = NA_HALO_ROWS * GRID_W
    hper = NA_ROWS_PER_STEP // NA_HALO_ROWS
    nh = rows // NA_HALO_ROWS

    def spec(col, shift):
        if shift == 0:
            return pl.BlockSpec((1, blk, NA_WIDTH), lambda b, i: (b, i, col))
        if shift < 0:
            return pl.BlockSpec((1, hblk, NA_WIDTH), lambda b, i: (b, jnp.maximum(i * hper - 1, 0), col))
        return pl.BlockSpec((1, hblk, NA_WIDTH), lambda b, i: (b, jnp.minimum((i + 1) * hper, nh - 1), col))

    return pl.pallas_call(
        functools.partial(_na_kernel, rows=rows),
        grid=(B, nblk),
        in_specs=[spec(0, 0), spec(1, -1), spec(1, 0), spec(1, 1), spec(2, -1), spec(2, 0), spec(2, 1),
                  _const_spec(na_m.shape), _const_spec(tb.shape)],
        out_specs=pl.BlockSpec((1, blk, NA_WIDTH), lambda b, i: (b, i, 0)),
        out_shape=jax.ShapeDtypeStruct((B, S, NA_WIDTH), BF16),
        scratch_shapes=[pltpu.VMEM((blk + 2 * hblk, NA_WIDTH), BF16), pltpu.VMEM((blk + 2 * hblk, 2 * NA_WIDTH), BF16),
                        pltpu.VMEM((N_META, 2 * NA_WIDTH), BF16)],
        compiler_params=_params("parallel", "arbitrary"),
        name="nattn",
    )(na3, na3, na3, na3, na3, na3, na3, na_m, tb)


DN_TILE = 512
DN_SCAN_CHUNKS = 2
DN_SCAN_BATCH = 8
QUAD = DN_HEADS * CHUNK
GATE_B = 0
GATE_G = 2 * DN_HEADS
HALO = 8


def _dot_hilo(lhs_exact, x):
    hi = x.astype(BF16)
    lo = (x - hi.astype(F32)).astype(BF16)
    return _dot(lhs_exact, hi) + _dot(lhs_exact, lo)


def _hilo_dot(x, rhs_exact):
    hi = x.astype(BF16)
    lo = (x - hi.astype(F32)).astype(BF16)
    return _dot(hi, rhs_exact) + _dot(lo, rhs_exact)


def _conv_act(ext, cw, n):
    nv = n // HALO
    x3 = ext[...].reshape(nv + 2, HALO, ext.shape[1])
    sub = lax.broadcasted_iota(jnp.int32, (nv, HALO, ext.shape[1]), 1)
    acc = x3[1:nv + 1] * cw[CONV_W // 2]
    for j in range(CONV_W):
        d = j - CONV_W // 2
        if d == 0:
            continue
        r = pltpu.roll(x3, (-d) % HALO, 1)
        if d > 0:
            shifted = jnp.where(sub < HALO - d, r[1:nv + 1], r[2:nv + 2])
        else:
            shifted = jnp.where(sub >= -d, r[1:nv + 1], r[0:nv])
        acc = acc + shifted * cw[j]
    acc = acc.reshape(n, ext.shape[1])
    return acc * jax.nn.sigmoid(acc)


def _l2n(x):
    return x * lax.rsqrt(jnp.sum(x * x, axis=-1, keepdims=True) + EPS)


def _normalize_heads(y):
    qs, ks = [], []
    for h in range(DN_HEADS):
        qs.append(_l2n(y[:, h * DN_DIM:(h + 1) * DN_DIM]) * (DN_DIM ** -0.5))
        ks.append(_l2n(y[:, DN_WIDTH + h * DN_DIM:DN_WIDTH + (h + 1) * DN_DIM]))
    return jnp.concatenate(qs, axis=-1), jnp.concatenate(ks, axis=-1), y[:, 2 * DN_WIDTH:]


def _gate_tile(ba, gpar):
    lane = lax.broadcasted_iota(jnp.int32, ba.shape, 1)
    xg = ba + gpar[1:2, :]
    softplus = jnp.maximum(xg, 0.0) + jnp.log1p(jnp.exp(-jnp.abs(xg)))
    return jnp.where(lane < GATE_G, jax.nn.sigmoid(ba), gpar[0:1, :] * softplus)


def _blockdiag(x, nblk, col_blk):
    rows, width = x.shape
    if col_blk % LANES == 0:
        zero = jnp.zeros((rows, col_blk), x.dtype)
        return jnp.concatenate(
            [jnp.concatenate([x[:, j * col_blk:(j + 1) * col_blk] if j % nblk == u else zero
                              for j in range(width // col_blk)], axis=1) for u in range(nblk)], axis=0)
    t = jnp.concatenate([x] * nblk, axis=0)
    rb = lax.broadcasted_iota(jnp.int32, t.shape, 0) // rows
    cb = (lax.broadcasted_iota(jnp.int32, t.shape, 1) // col_blk) % nblk
    return jnp.where(rb == cb, t, jnp.zeros_like(t))


def _tile_cumsums(g_tile, n):
    r = lax.broadcasted_iota(jnp.int32, (n, n), 0)
    c = lax.broadcasted_iota(jnp.int32, (n, n), 1)
    same = (r // CHUNK) == (c // CHUNK)
    allc = jnp.where(same, 1.0, 0.0)
    lower = jnp.where(c <= r, allc, 0.0).astype(BF16)
    upper = jnp.where(c >= r, allc, 0.0).astype(BF16)
    allc = allc.astype(BF16)
    lane = lax.broadcasted_iota(jnp.int32, g_tile.shape, 1)
    gc = jnp.where(lane < GATE_G + DN_HEADS, _dot_hilo(lower, g_tile), _dot_hilo(upper, g_tile))
    tot = _dot_hilo(allc, g_tile)
    return gc, tot


def _lockstep(gens):
    results = [None] * len(gens)
    alive = list(range(len(gens)))
    while alive:
        still = []
        for i in alive:
            try:
                next(gens[i])
                still.append(i)
            except StopIteration as stop:
                results[i] = stop.value
        alive = still
    return results


def _decay_tiles(gate, n):
    gc, tot = _tile_cumsums(gate, n)
    return gc, jnp.exp(gc), jnp.exp(tot - gc), jnp.exp(tot)


def _chunk_wy(q, k, v, gate, gc, etile, ektile, cdtile, direction, shared):
    rr = lax.broadcasted_iota(jnp.int32, (CHUNK, QUAD), 0)
    cc = lax.broadcasted_iota(jnp.int32, (CHUNK, QUAD), 1) % CHUNK
    if direction == 0:
        incl, strict = rr >= cc, rr > cc
    else:
        incl, strict = rr <= cc, rr < cc
    eye4 = rr == cc
    col0 = direction * DN_HEADS
    qk_parts, kk_parts = [], []
    lane256 = lax.broadcasted_iota(jnp.int32, (CHUNK, 2 * DN_DIM), 1)
    for p in range(DN_HEADS // 2 if "qk4" not in shared else 0):
        sl = slice(2 * p * DN_DIM, (2 * p + 2) * DN_DIM)
        kp = k[:, sl].astype(BF16)
        zero = jnp.zeros_like(kp)
        rhs_t = jnp.concatenate([jnp.where(lane256 < DN_DIM, kp, zero), jnp.where(lane256 < DN_DIM, zero, kp)],
                                axis=0)
        lhs = jnp.concatenate([q[:, sl].astype(BF16), kp], axis=0)
        prod = lax.dot_general(lhs, rhs_t, (((1,), (1,)), ((), ())), preferred_element_type=F32)
        qk_parts.append(prod[:CHUNK])
        kk_parts.append(prod[CHUNK:])
    if qk_parts:
        shared["qk4"] = jnp.concatenate(qk_parts, axis=-1)
        shared["kk4"] = jnp.concatenate(kk_parts, axis=-1)
    qk4, kk4 = shared["qk4"], shared["kk4"]
    yield
    lane_sel = lax.broadcasted_iota(jnp.int32, (LANES, QUAD), 0)
    quad_head = lax.broadcasted_iota(jnp.int32, (LANES, QUAD), 1) // CHUNK
    sel_g = jnp.where(lane_sel == GATE_G + col0 + quad_head, 1.0, 0.0).astype(BF16)
    sel_b = jnp.where(lane_sel == GATE_B + col0 + quad_head, 1.0, 0.0).astype(BF16)
    colmat = _hilo_dot(gc, sel_g)
    beta4 = _hilo_dot(gate, sel_b)
    yield
    rowmat = _dot_hilo(jnp.ones((CHUNK, CHUNK), BF16), jnp.where(eye4, colmat, 0.0))
    yield
    dmat = jnp.where(incl, jnp.exp(jnp.where(incl, colmat - rowmat, 0.0)), 0.0)
    nmat = jnp.where(strict, -(beta4 * kk4 * dmat), 0.0)
    t4 = jnp.where(eye4, 1.0, 0.0) + nmat
    nb16 = nmat.astype(BF16)
    p4 = _dot(nb16, _blockdiag(nb16, DN_HEADS, CHUNK))
    yield
    for _ in range(int(np.log2(CHUNK)) - 2):
        both = _dot(jnp.concatenate([t4, p4], axis=0).astype(BF16), _blockdiag(p4.astype(BF16), DN_HEADS, CHUNK))
        t4 = t4 + both[:CHUNK]
        p4 = both[CHUNK:]
        yield
    t4 = t4 + _dot(t4.astype(BF16), _blockdiag(p4.astype(BF16), DN_HEADS, CHUNK))
    yield

    def wide_cols(tile, base):
        return jnp.concatenate(
            [jnp.broadcast_to(tile[:, base + col0 + h:base + col0 + h + 1], (CHUNK, DN_DIM))
             for h in range(DN_HEADS)], axis=-1)

    beta_w = wide_cols(gate, GATE_B)
    e_w = wide_cols(etile, GATE_G)
    vb = v * beta_w
    kbg = k * beta_w * e_w
    q_dec = (q * e_w).astype(BF16)
    k_dec = k * wide_cols(ektile, GATE_G)
    cd_row = wide_cols(cdtile, GATE_G)[0:1, :]
    rhs = jnp.concatenate([vb, kbg], axis=-1).astype(BF16)
    sol = _dot(t4.astype(BF16), _blockdiag(rhs, DN_HEADS, DN_DIM))
    u = sol[:, :DN_WIDTH]
    w = sol[:, DN_WIDTH:].astype(BF16)
    qk_out = (qk4 * dmat).astype(BF16)
    kd_t = []
    for p in range(DN_HEADS // 2):
        pair = jnp.concatenate([k_dec[:, 2 * p * DN_DIM:(2 * p + 1) * DN_DIM],
                                k_dec[:, (2 * p + 1) * DN_DIM:(2 * p + 2) * DN_DIM]], axis=0)
        kd_t.append(pair.T)
    kd_t = jnp.concatenate(kd_t, axis=-1).astype(BF16)
    return u, w, q_dec, qk_out, kd_t, cd_row


def _dn_prep_kernel(cur_ref, prev_ref, next_ref, ba_ref, meta_ref, cw_ref, gpar_ref,
                    u_ref, w_ref, qd_ref, qk_ref, kdt_ref, cd_ref,
                    ext_s, q_s, k_s, v_s, gate_s, gc_s, e_s, ek_s, cdt_s):
    t = pl.program_id(1)
    n = DN_TILE
    nqkv = 3 * DN_WIDTH
    prev = jnp.where(t == 0, meta_ref[N_META - HALO:N_META, :nqkv], prev_ref[0])
    nxt = jnp.where(t == pl.num_programs(1) - 1, jnp.zeros((HALO, nqkv), F32), next_ref[0])
    ext_s[0:HALO, :] = prev
    ext_s[HALO:HALO + n, :] = cur_ref[0]
    ext_s[HALO + n:2 * HALO + n, :] = nxt
    q, k, v = _normalize_heads(_conv_act(ext_s, cw_ref[...], n))
    q_s[...] = q
    k_s[...] = k
    v_s[...] = v
    gate = _gate_tile(ba_ref[0], gpar_ref[...])
    gate_s[...] = gate
    gc_s[...], e_s[...], ek_s[...], cdt_s[...] = _decay_tiles(gate, n)

    units = [(c, d) for c in range(n // CHUNK) for d in range(2)]
    gens = []
    shared = [{} for _ in range(n // CHUNK)]
    for c, d in units:
        rows = slice(c * CHUNK, (c + 1) * CHUNK)
        gens.append(_chunk_wy(q_s[rows, :], k_s[rows, :], v_s[rows, :],
                              gate_s[rows, :], gc_s[rows, :], e_s[rows, :], ek_s[rows, :], cdt_s[rows, :],
                              d, shared[c]))
    for (c, d), (u, w, qd, qk, kdt, cd) in zip(units, _lockstep(gens)):
        rows = slice(c * CHUNK, (c + 1) * CHUNK)
        u_ref[d, 0, rows, :] = u
        w_ref[d, 0, rows, :] = w
        qd_ref[d, 0, rows, :] = qd
        qk_ref[d, 0, rows, :] = qk
        kdt_ref[d, 0, 2 * c * CHUNK:2 * (c + 1) * CHUNK, :] = kdt
        cd_ref[d, 0, c, 0, :, :] = cd


def _dn_prep(dn_x, ba_x, dn_m, cw, gpar, B, S):
    n = DN_TILE
    nch = n // CHUNK
    nqkv = 3 * DN_WIDTH
    dn3 = dn_x.reshape(B, S, dn_x.shape[-1])
    ba3 = ba_x.reshape(B, S, LANES)
    hb = n // HALO
    nhb = S // HALO
    outs = [
        ((2, B, S, DN_WIDTH), F32, (2, 1, n, DN_WIDTH)),
        ((2, B, S, DN_WIDTH), BF16, (2, 1, n, DN_WIDTH)),
        ((2, B, S, DN_WIDTH), BF16, (2, 1, n, DN_WIDTH)),
        ((2, B, S, QUAD), BF16, (2, 1, n, QUAD)),
        ((2, B, 2 * S, QUAD), BF16, (2, 1, 2 * n, QUAD)),
    ]
    out_specs = [pl.BlockSpec(blk, lambda b, t: (0, b, t, 0)) for _, _, blk in outs]
    out_shape = [jax.ShapeDtypeStruct(shp, dt) for shp, dt, _ in outs]
    out_specs.append(pl.BlockSpec((2, 1, nch, 1, 1, DN_WIDTH), lambda b, t: (0, b, t, 0, 0, 0)))
    out_shape.append(jax.ShapeDtypeStruct((2, B, S // CHUNK, 1, 1, DN_WIDTH), F32))
    return pl.pallas_call(
        _dn_prep_kernel,
        grid=(B, S // n),
        in_specs=[
            pl.BlockSpec((1, n, nqkv), lambda b, t: (b, t, 0)),
            pl.BlockSpec((1, HALO, nqkv), lambda b, t: (b, jnp.maximum(t * hb - 1, 0), 0)),
            pl.BlockSpec((1, HALO, nqkv), lambda b, t: (b, jnp.minimum((t + 1) * hb, nhb - 1), 0)),
            pl.BlockSpec((1, n, LANES), lambda b, t: (b, t, 0)),
            _const_spec(dn_m.shape), _const_spec(cw.shape), _const_spec(gpar.shape),
        ],
        out_specs=out_specs,
        out_shape=out_shape,
        scratch_shapes=[pltpu.VMEM((n + 2 * HALO, nqkv), F32)] + [pltpu.VMEM((n, DN_WIDTH), F32)] * 3
        + [pltpu.VMEM((n, LANES), F32)] * 5,
        compiler_params=_params("parallel", "parallel"),
        name="dn_prep",
    )(dn3, dn3, dn3, ba3, dn_m, cw, gpar)


def _state_step(s_ref, u, w, qd, qk, kdt, cd):
    s = s_ref[...]
    ws_parts, qs_parts = [], []
    for p in range(DN_HEADS // 2):
        sl = slice(2 * p * DN_DIM, (2 * p + 2) * DN_DIM)
        bd = _blockdiag(s[:, sl].astype(BF16), 2, DN_DIM)
        both = _dot(jnp.concatenate([w[:, sl], qd[:, sl]], axis=0), bd)
        ws_parts.append(both[:CHUNK])
        qs_parts.append(both[CHUNK:])
    yield
    v_new = u - jnp.concatenate(ws_parts, axis=-1)
    both = _dot(jnp.concatenate([qk, kdt], axis=0), _blockdiag(v_new.astype(BF16), DN_HEADS, DN_DIM))
    yield
    s_ref[...] = s * cd + both[CHUNK:]
    return jnp.concatenate(qs_parts, axis=-1) + both[:CHUNK]


def _dn_scan_kernel(sinit_ref, uf, wf, qdf, qkf, kdtf, cdf, ub, wb, qdb, qkb, kdtb, cdb,
                    of_ref, ob_ref, sf, sb, *, nch, nbatch):
    @pl.when(pl.program_id(1) == 0)
    def _():
        sf[...] = sinit_ref[...]
        sb[...] = jnp.zeros_like(sb)

    def body(c, carry):
        cb = nch - 1 - c
        rf = pl.ds(pl.multiple_of(c * CHUNK, CHUNK), CHUNK)
        rb = pl.ds(pl.multiple_of(cb * CHUNK, CHUNK), CHUNK)
        rf2 = pl.ds(pl.multiple_of(c * 2 * CHUNK, 2 * CHUNK), 2 * CHUNK)
        rb2 = pl.ds(pl.multiple_of(cb * 2 * CHUNK, 2 * CHUNK), 2 * CHUNK)
        gens = []
        for e in range(nbatch):
            gens.append(_state_step(sf.at[e], uf[0, e, rf, :], wf[0, e, rf, :], qdf[0, e, rf, :], qkf[0, e, rf, :],
                                    kdtf[0, e, rf2, :], cdf[0, e, c, 0]))
            gens.append(_state_step(sb.at[e], ub[0, e, rb, :], wb[0, e, rb, :], qdb[0, e, rb, :], qkb[0, e, rb, :],
                                    kdtb[0, e, rb2, :], cdb[0, e, cb, 0]))
        outs = _lockstep(gens)
        for e in range(nbatch):
            of_ref[e, rf, :] = outs[2 * e]
            ob_ref[e, rb, :] = outs[2 * e + 1]
        return carry

    lax.fori_loop(0, nch, body, 0)


def _dn_scan(sinit, u, w, qd, qk, kdt, cd, B, S):
    nch = DN_SCAN_CHUNKS
    n = nch * CHUNK
    nb = S // n
    nbatch = DN_SCAN_BATCH if B % DN_SCAN_BATCH == 0 else 1

    def specs(d):
        im = (lambda b, i: (d, b, i, 0)) if d == 0 else (lambda b, i: (d, b, nb - 1 - i, 0))
        im6 = (lambda b, i: (d, b, i, 0, 0, 0)) if d == 0 else (lambda b, i: (d, b, nb - 1 - i, 0, 0, 0))
        return [pl.BlockSpec((1, nbatch, n, DN_WIDTH), im), pl.BlockSpec((1, nbatch, n, DN_WIDTH), im),
                pl.BlockSpec((1, nbatch, n, DN_WIDTH), im), pl.BlockSpec((1, nbatch, n, QUAD), im),
                pl.BlockSpec((1, nbatch, 2 * n, QUAD), im), pl.BlockSpec((1, nbatch, nch, 1, 1, DN_WIDTH), im6)]

    return pl.pallas_call(
        functools.partial(_dn_scan_kernel, nch=nch, nbatch=nbatch),
        grid=(B // nbatch, nb),
        in_specs=[pl.BlockSpec((nbatch, DN_DIM, DN_WIDTH), lambda b, i: (b, 0, 0))] + specs(0) + specs(1),
        out_specs=[pl.BlockSpec((nbatch, n, DN_WIDTH), lambda b, i: (b, i, 0)),
                   pl.BlockSpec((nbatch, n, DN_WIDTH), lambda b, i: (b, nb - 1 - i, 0))],
        out_shape=[jax.ShapeDtypeStruct((B, S, DN_WIDTH), F32)] * 2,
        scratch_shapes=[pltpu.VMEM((nbatch, DN_DIM, DN_WIDTH), F32)] * 2,
        compiler_params=_params("parallel", "arbitrary"),
        name="dn_scan",
    )(sinit, u, w, qd, qk, kdt, cd, u, w, qd, qk, kdt, cd)


def _dn_meta_kernel(x0_ref, meta_ref, bam_ref, cw_ref, gpar_ref, sinit_ref, ext_s):
    nqkv = 3 * DN_WIDTH
    ext_s[0:HALO, :] = jnp.zeros((HALO, nqkv), F32)
    ext_s[HALO:HALO + N_META, :] = meta_ref[:, :nqkv]
    ext_s[HALO + N_META:, :] = x0_ref[0]
    q, k, v = _normalize_heads(_conv_act(ext_s, cw_ref[...], N_META))
    pad = CHUNK - N_META
    zw = jnp.zeros((pad, DN_WIDTH), F32)
    q, k, v = (jnp.concatenate([zw, a], axis=0) for a in (q, k, v))
    gate = jnp.concatenate([jnp.zeros((pad, LANES), F32), _gate_tile(bam_ref[...], gpar_ref[...])], axis=0)
    gc, e, ek, cdt = _decay_tiles(gate, CHUNK)
    (u, _, _, _, kdt, _), = _lockstep([_chunk_wy(q, k, v, gate, gc, e, ek, cdt, 0, {})])
    sinit_ref[0] = _dot(kdt, _blockdiag(u.astype(BF16), DN_HEADS, DN_DIM))


def _dn_meta(dn_x, dn_m, ba_m, cw, gpar, B, S):
    nqkv = 3 * DN_WIDTH
    dn3 = dn_x.reshape(B, S, dn_x.shape[-1])
    return pl.pallas_call(
        _dn_meta_kernel,
        grid=(B,),
        in_specs=[pl.BlockSpec((1, HALO, nqkv), lambda b: (b, 0, 0)),
                  _const_spec(dn_m.shape), _const_spec(ba_m.shape), _const_spec(cw.shape), _const_spec(gpar.shape)],
        out_specs=pl.BlockSpec((1, DN_DIM, DN_WIDTH), lambda b: (b, 0, 0)),
        out_shape=jax.ShapeDtypeStruct((B, DN_DIM, DN_WIDTH), F32),
        scratch_shapes=[pltpu.VMEM((2 * HALO + N_META, nqkv), F32)],
        compiler_params=_params("parallel"),
        name="dn_meta",
    )(dn3, dn_m, ba_m, cw, gpar)


def _gate_params(a_log, dt_bias):
    neg_a = jnp.zeros((LANES,), F32).at[GATE_G:GATE_G + 2 * DN_HEADS].set(-jnp.exp(a_log.astype(F32)).reshape(-1))
    dtb = jnp.zeros((LANES,), F32).at[GATE_G:GATE_G + 2 * DN_HEADS].set(dt_bias.astype(F32).reshape(-1))
    return jnp.concatenate([neg_a[None], dtb[None], jnp.zeros((6, LANES), F32)], axis=0)


def _deltanet(dn_x, ba_x, dn_m, ba_m, conv_w, a_log, dt_bias, B, S):
    gpar = _gate_params(a_log, dt_bias)
    cw = jnp.pad(conv_w.astype(F32), ((0, 8 - CONV_W), (0, 0)))
    u, w, qd, qk, kdt, cd = _dn_prep(dn_x, ba_x, dn_m, cw, gpar, B, S)
    sinit = _dn_meta(dn_x, dn_m, ba_m, cw, gpar, B, S)
    return _dn_scan(sinit, u, w, qd, qk, kdt, cd, B, S)


INPROJ_TM = 1024


def kernel(x, meta_tokens, g_mix, w_in, na_rel_bias, dn_conv_w, dn_a_log, dn_dt_bias, dn_norm_g,
           w_out, g_ffn, w_gate, w_up, w_down, g_final):
    B, S, D = x.shape
    l = 0
    nqkv = 3 * NA_WIDTH
    ndn = 4 * DN_WIDTH
    w = w_in[l]
    wna = jnp.concatenate([w[:, :NA_WIDTH] * (NA_HEAD_DIM ** -0.5), w[:, NA_WIDTH:nqkv]], axis=1).astype(BF16)
    wdn = w[:, nqkv:nqkv + ndn].astype(BF16)
    wba = jnp.pad(w[:, nqkv + ndn:], ((0, 0), (0, LANES - 4 * DN_HEADS))).astype(BF16)
    gm = g_mix[l][None, :]

    x2d = x.reshape(B * S, D)
    na_x, dn_x, ba_x = _inproj(x2d, gm, wna, wdn, wba, tm=INPROJ_TM)
    na_m, dn_m, ba_m = _inproj(meta_tokens, gm, wna, wdn, wba, tm=N_META)

    y_na = _na(na_x, na_m, _na_bias_table(na_rel_bias[l]), B, S)
    o_f, o_b = _deltanet(dn_x, ba_x, dn_m, ba_m, dn_conv_w[l], dn_a_log[l], dn_dt_bias[l], B, S)

    wo = w_out[l].astype(BF16)
    out = _tail(x2d, y_na.reshape(B * S, NA_WIDTH), o_f.reshape(B * S, DN_WIDTH), o_b.reshape(B * S, DN_WIDTH),
                dn_x, dn_norm_g[l][None, :], wo[:NA_WIDTH], wo[NA_WIDTH:], g_ffn[l][None, :],
                w_gate[l].astype(BF16), w_up[l].astype(BF16), w_down[l].astype(BF16),
                g_final[None, :], tm=512, ff_chunk=256)
    return out.reshape(B, S, D)
```

```python
import functools

import jax
import jax.numpy as jnp
import numpy as np
from jax import lax
from jax.experimental import pallas as pl
from jax.experimental.pallas import tpu as pltpu

F32 = jnp.float32
BF16 = jnp.bfloat16

GRID_W = 64
N_META = 16
NA_HEADS = 8
NA_HEAD_DIM = 64
NA_WIDTH = NA_HEADS * NA_HEAD_DIM
WIN_ROWS = 8
WIN_COLS = 16
DN_HEADS = 4
DN_DIM = 128
DN_WIDTH = DN_HEADS * DN_DIM
CONV_W = 5
CHUNK = 64
EPS = 1e-6
NEG = -1e30
LANES = 128

VMEM_LIMIT = 56 * 1024 * 1024


def _params(*sem):
    return pltpu.CompilerParams(dimension_semantics=sem, vmem_limit_bytes=VMEM_LIMIT)


def _const_spec(shape):
    nd = len(shape)
    return pl.BlockSpec(shape, lambda *_: (0,) * nd, pipeline_mode=pl.Buffered(1))


def _rms_scale(x, g):
    ms = jnp.mean(x * x, axis=-1, keepdims=True)
    return x * lax.rsqrt(ms + EPS) * g


def _dot(a, b):
    return jnp.dot(a, b, preferred_element_type=F32)


PROJ_COLS = 512


def _inproj_kernel(x_ref, g_ref, wna_ref, wdn_ref, wba_ref, na_ref, dn_ref, ba_ref):
    u = _rms_scale(x_ref[...], g_ref[...]).astype(BF16)
    for c in range(wna_ref.shape[1] // PROJ_COLS):
        sl = slice(c * PROJ_COLS, (c + 1) * PROJ_COLS)
        na_ref[:, sl] = _dot(u, wna_ref[:, sl]).astype(na_ref.dtype)
    for c in range(wdn_ref.shape[1] // PROJ_COLS):
        sl = slice(c * PROJ_COLS, (c + 1) * PROJ_COLS)
        dn_ref[:, sl] = _dot(u, wdn_ref[:, sl])
    ba_ref[...] = _dot(u, wba_ref[...])


def _inproj(x2d, g, wna, wdn, wba, tm):
    m, d = x2d.shape
    return pl.pallas_call(
        _inproj_kernel,
        grid=(m // tm,),
        in_specs=[
            pl.BlockSpec((tm, d), lambda i: (i, 0)),
            _const_spec(g.shape),
            _const_spec(wna.shape),
            _const_spec(wdn.shape),
            _const_spec(wba.shape),
        ],
        out_specs=[
            pl.BlockSpec((tm, wna.shape[1]), lambda i: (i, 0)),
            pl.BlockSpec((tm, wdn.shape[1]), lambda i: (i, 0)),
            pl.BlockSpec((tm, wba.shape[1]), lambda i: (i, 0)),
        ],
        out_shape=[
            jax.ShapeDtypeStruct((m, wna.shape[1]), BF16),
            jax.ShapeDtypeStruct((m, wdn.shape[1]), F32),
            jax.ShapeDtypeStruct((m, wba.shape[1]), F32),
        ],
        compiler_params=_params("parallel"),
        name="inproj",
    )(x2d, g, wna, wdn, wba)


def _tail_kernel(x_ref, yna_ref, of_ref, ob_ref, z_ref, gdn_ref, wona_ref, wodn_ref, gffn_ref,
                 wg_ref, wu_ref, wd_ref, gfin_ref, out_ref, *, ff_chunk):
    o = of_ref[...] + ob_ref[...]
    z = z_ref[...]
    gdn = gdn_ref[...]
    ydn = []
    for h in range(DN_HEADS):
        sl = slice(h * DN_DIM, (h + 1) * DN_DIM)
        ydn.append(_rms_scale(o[:, sl], gdn) * (z[:, sl] * jax.nn.sigmoid(z[:, sl])))
    ydn = jnp.concatenate(ydn, axis=-1).astype(BF16)
    h1 = x_ref[...]
    h1 = h1 + _dot(yna_ref[...], wona_ref[...])
    h1 = h1 + _dot(ydn, wodn_ref[...])
    u = _rms_scale(h1, gffn_ref[...]).astype(BF16)
    out_ref[...] = h1
    for c in range(wg_ref.shape[1] // ff_chunk):
        sl = slice(c * ff_chunk, (c + 1) * ff_chunk)
        gate = _dot(u, wg_ref[:, sl])
        up = _dot(u, wu_ref[:, sl])
        act = (gate * jax.nn.sigmoid(gate) * up).astype(BF16)
        out_ref[...] += _dot(act, wd_ref[sl, :])
    out_ref[...] = _rms_scale(out_ref[...], gfin_ref[...])


def _tail(x2d, yna, o_f, o_b, z, gdn, wona, wodn, gffn, wg, wu, wd, gfin, tm, ff_chunk):
    m, d = x2d.shape
    assert m % tm == 0 and wg.shape[1] % ff_chunk == 0
    row = lambda w: pl.BlockSpec((tm, w), lambda i: (i, 0))
    consts = (gdn, wona, wodn, gffn, wg, wu, wd, gfin)
    return pl.pallas_call(
        functools.partial(_tail_kernel, ff_chunk=ff_chunk),
        grid=(m // tm,),
        in_specs=[row(d), row(yna.shape[1]), row(o_f.shape[1]), row(o_b.shape[1]),
                  pl.BlockSpec((tm, DN_WIDTH), lambda i: (i, 3))]
        + [_const_spec(c.shape) for c in consts],
        out_specs=row(d),
        out_shape=jax.ShapeDtypeStruct((m, d), F32),
        compiler_params=_params("parallel"),
        name="tail",
    )(x2d, yna, o_f, o_b, z, *consts)


NA_ROWS_PER_STEP = 32
NA_HALO_ROWS = 8
NA_PAIRS = NA_WIDTH // LANES


def _na_bias_table(rel_bias):
    c = np.arange(GRID_W)
    col_start = np.clip(c - WIN_COLS // 2, 0, GRID_W - WIN_COLS)
    col_in = (c[None, :] >= col_start[:, None]) & (c[None, :] < col_start[:, None] + WIN_COLS)
    dc_idx = np.clip(c[None, :] - c[:, None], 1 - WIN_COLS, WIN_COLS - 1) + (WIN_COLS - 1)
    onehot = (dc_idx[None] == np.arange(2 * WIN_COLS - 1)[:, None, None]) & col_in[None]
    onehot = jnp.asarray(onehot.reshape(2 * WIN_COLS - 1, GRID_W * GRID_W), F32)
    rb = rel_bias.astype(F32)
    rows_sel = jnp.stack([rb[:, WIN_ROWS - 1 - o:2 * WIN_ROWS - 1 - o, :] for o in range(WIN_ROWS)])
    t = jnp.dot(rows_sel.reshape(-1, 2 * WIN_COLS - 1), onehot, precision=lax.Precision.HIGHEST)
    t = t.reshape(WIN_ROWS, NA_HEADS, WIN_ROWS, GRID_W, GRID_W)
    t = jnp.where(col_in[None, None, None], t, NEG)
    t = jnp.transpose(t, (0, 1, 3, 2, 4))
    return t.reshape(WIN_ROWS, NA_HEADS, GRID_W, WIN_ROWS * GRID_W)


def _na_kernel(q_ref, kp_ref, kc_ref, kn_ref, vp_ref, vc_ref, vn_ref, meta_ref, tb_ref, out_ref,
               kbuf, vbuf, vmbuf, *, rows):
    i = pl.program_id(1)
    blk = NA_ROWS_PER_STEP * GRID_W
    hblk = NA_HALO_ROWS * GRID_W
    nkey = WIN_ROWS * GRID_W
    lo = 0
    for kr, vr, size in ((kp_ref, vp_ref, hblk), (kc_ref, vc_ref, blk), (kn_ref, vn_ref, hblk)):
        kbuf[lo:lo + size, :] = kr[0]
        for t in range(NA_PAIRS):
            vbuf[lo:lo + size, 2 * t * LANES:(2 * t + 1) * LANES] = vr[0, :, t * LANES:(t + 1) * LANES]
            vbuf[lo:lo + size, (2 * t + 1) * LANES:(2 * t + 2) * LANES] = jnp.ones((size, LANES), BF16)
        lo += size
    for t in range(NA_PAIRS):
        vmbuf[:, 2 * t * LANES:(2 * t + 1) * LANES] = meta_ref[:, 2 * NA_WIDTH + t * LANES:2 * NA_WIDTH + (t + 1) * LANES]
        vmbuf[:, (2 * t + 1) * LANES:(2 * t + 2) * LANES] = jnp.ones((N_META, LANES), BF16)
    nt = (((1,), (1,)), ((), ()))
    lane = lax.broadcasted_iota(jnp.int32, (GRID_W, LANES), 1)
    first = lane < NA_HEAD_DIM

    def row_body(j, carry):
        r = i * NA_ROWS_PER_STEP + j
        start = jnp.clip(r - WIN_ROWS // 2, 0, rows - WIN_ROWS)
        o = r - start
        off = pl.multiple_of((start - (i * NA_ROWS_PER_STEP - NA_HALO_ROWS)) * GRID_W, GRID_W)
        qoff = pl.multiple_of(j * GRID_W, GRID_W)
        for t in range(NA_PAIRS):
            ls = slice(t * LANES, (t + 1) * LANES)
            qp = q_ref[0, pl.ds(qoff, GRID_W), ls]
            zero = jnp.zeros_like(qp)
            qs = jnp.concatenate([jnp.where(first, qp, zero), jnp.where(first, zero, qp)], axis=0)
            s = lax.dot_general(qs, kbuf[pl.ds(off, nkey), ls], nt, preferred_element_type=F32)
            s = s + tb_ref[o, t]
            sm = lax.dot_general(qs, meta_ref[:, NA_WIDTH + t * LANES:NA_WIDTH + (t + 1) * LANES], nt,
                                 preferred_element_type=F32)
            m = jnp.maximum(jnp.max(s, axis=-1, keepdims=True), jnp.max(sm, axis=-1, keepdims=True))
            p = jnp.exp(s - m).astype(BF16)
            pm = jnp.exp(sm - m).astype(BF16)
            acc = _dot(p, vbuf[pl.ds(off, nkey), 2 * t * LANES:(2 * t + 2) * LANES])
            acc = acc + _dot(pm, vmbuf[:, 2 * t * LANES:(2 * t + 2) * LANES])
            res = acc[:, :LANES] / acc[:, LANES:]
            out_ref[0, pl.ds(qoff, GRID_W), ls] = jnp.where(first, res[:GRID_W], res[GRID_W:]).astype(out_ref.dtype)
        return carry

    lax.fori_loop(0, NA_ROWS_PER_STEP, row_body, 0, unroll=True)


def _na(na_x, na_m, tb, B, S):
    rows = S // GRID_W
    nblk = rows // NA_ROWS_PER_STEP
    blk = NA_ROWS_PER_STEP * GRID_W
    na3 = na_x.reshape(B, S, 3 * NA_WIDTH)
    tb = tb.reshape(WIN_ROWS, NA_PAIRS, 2 * GRID_W, WIN_ROWS * GRID_W)

    hblk = NA_HALO_ROWS * GRID_W
    hper = NA_ROWS_PER_STEP // NA_HALO_ROWS
    nh = rows // NA_HALO_ROWS

    def spec(col, shift):
        if shift == 0:
            return pl.BlockSpec((1, blk, NA_WIDTH), lambda b, i: (b, i, col))
        if shift < 0:
            return pl.BlockSpec((1, hblk, NA_WIDTH), lambda b, i: (b, jnp.maximum(i * hper - 1, 0), col))
        return pl.BlockSpec((1, hblk, NA_WIDTH), lambda b, i: (b, jnp.minimum((i + 1) * hper, nh - 1), col))

    return pl.pallas_call(
        functools.partial(_na_kernel, rows=rows),
        grid=(B, nblk),
        in_specs=[spec(0, 0), spec(1, -1), spec(1, 0), spec(1, 1), spec(2, -1), spec(2, 0), spec(2, 1),
                  _const_spec(na_m.shape), _const_spec(tb.shape)],
        out_specs=pl.BlockSpec((1, blk, NA_WIDTH), lambda b, i: (b, i, 0)),
        out_shape=jax.ShapeDtypeStruct((B, S, NA_WIDTH), BF16),
        scratch_shapes=[pltpu.VMEM((blk + 2 * hblk, NA_WIDTH), BF16), pltpu.VMEM((blk + 2 * hblk, 2 * NA_WIDTH), BF16),
                        pltpu.VMEM((N_META, 2 * NA_WIDTH), BF16)],
        compiler_params=_params("parallel", "arbitrary"),
        name="nattn",
    )(na3, na3, na3, na3, na3, na3, na3, na_m, tb)


DN_TILE = 512
DN_SCAN_CHUNKS = 2
DN_SCAN_BATCH = 8
QUAD = DN_HEADS * CHUNK
GATE_B = 0
GATE_G = 2 * DN_HEADS
HALO = 8


def _dot_hilo(lhs_exact, x):
    hi = x.astype(BF16)
    lo = (x - hi.astype(F32)).astype(BF16)
    return _dot(lhs_exact, hi) + _dot(lhs_exact, lo)


def _hilo_dot(x, rhs_exact):
    hi = x.astype(BF16)
    lo = (x - hi.astype(F32)).astype(BF16)
    return _dot(hi, rhs_exact) + _dot(lo, rhs_exact)


def _conv_act(ext, cw, n):
    nv = n // HALO
    x3 = ext[...].reshape(nv + 2, HALO, ext.shape[1])
    sub = lax.broadcasted_iota(jnp.int32, (nv, HALO, ext.shape[1]), 1)
    acc = x3[1:nv + 1] * cw[CONV_W // 2]
    for j in range(CONV_W):
        d = j - CONV_W // 2
        if d == 0:
            continue
        r = pltpu.roll(x3, (-d) % HALO, 1)
        if d > 0:
            shifted = jnp.where(sub < HALO - d, r[1:nv + 1], r[2:nv + 2])
        else:
            shifted = jnp.where(sub >= -d, r[1:nv + 1], r[0:nv])
        acc = acc + shifted * cw[j]
    acc = acc.reshape(n, ext.shape[1])
    return acc * jax.nn.sigmoid(acc)


def _l2n(x):
    return x * lax.rsqrt(jnp.sum(x * x, axis=-1, keepdims=True) + EPS)


def _normalize_heads(y):
    qs, ks = [], []
    for h in range(DN_HEADS):
        qs.append(_l2n(y[:, h * DN_DIM:(h + 1) * DN_DIM]) * (DN_DIM ** -0.5))
        ks.append(_l2n(y[:, DN_WIDTH + h * DN_DIM:DN_WIDTH + (h + 1) * DN_DIM]))
    return jnp.concatenate(qs, axis=-1), jnp.concatenate(ks, axis=-1), y[:, 2 * DN_WIDTH:]


def _gate_tile(ba, gpar):
    lane = lax.broadcasted_iota(jnp.int32, ba.shape, 1)
    xg = ba + gpar[1:2, :]
    softplus = jnp.maximum(xg, 0.0) + jnp.log1p(jnp.exp(-jnp.abs(xg)))
    return jnp.where(lane < GATE_G, jax.nn.sigmoid(ba), gpar[0:1, :] * softplus)


def _blockdiag(x, nblk, col_blk):
    rows, width = x.shape
    if col_blk % LANES == 0:
        zero = jnp.zeros((rows, col_blk), x.dtype)
        return jnp.concatenate(
            [jnp.concatenate([x[:, j * col_blk:(j + 1) * col_blk] if j % nblk == u else zero
                              for j in range(width // col_blk)], axis=1) for u in range(nblk)], axis=0)
    t = jnp.concatenate([x] * nblk, axis=0)
    rb = lax.broadcasted_iota(jnp.int32, t.shape, 0) // rows
    cb = (lax.broadcasted_iota(jnp.int32, t.shape, 1) // col_blk) % nblk
    return jnp.where(rb == cb, t, jnp.zeros_like(t))


def _tile_cumsums(g_tile, n):
    r = lax.broadcasted_iota(jnp.int32, (n, n), 0)
    c = lax.broadcasted_iota(jnp.int32, (n, n), 1)
    same = (r // CHUNK) == (c // CHUNK)
    allc = jnp.where(same, 1.0, 0.0)
    lower = jnp.where(c <= r, allc, 0.0).astype(BF16)
    upper = jnp.where(c >= r, allc, 0.0).astype(BF16)
    allc = allc.astype(BF16)
    lane = lax.broadcasted_iota(jnp.int32, g_tile.shape, 1)
    gc = jnp.where(lane < GATE_G + DN_HEADS, _dot_hilo(lower, g_tile), _dot_hilo(upper, g_tile))
    tot = _dot_hilo(allc, g_tile)
    return gc, tot


def _lockstep(gens):
    results = [None] * len(gens)
    alive = list(range(len(gens)))
    while alive:
        still = []
        for i in alive:
            try:
                next(gens[i])
                still.append(i)
            except StopIteration as stop:
                results[i] = stop.value
        alive = still
    return results


def _decay_tiles(gate, n):
    gc, tot = _tile_cumsums(gate, n)
    return gc, jnp.exp(gc), jnp.exp(tot - gc), jnp.exp(tot)


def _chunk_wy(q, k, v, gate, gc, etile, ektile, cdtile, direction, shared):
    rr = lax.broadcasted_iota(jnp.int32, (CHUNK, QUAD), 0)
    cc = lax.broadcasted_iota(jnp.int32, (CHUNK, QUAD), 1) % CHUNK
    if direction == 0:
        incl, strict = rr >= cc, rr > cc
    else:
        incl, strict = rr <= cc, rr < cc
    eye4 = rr == cc
    col0 = direction * DN_HEADS
    qk_parts, kk_parts = [], []
    lane256 = lax.broadcasted_iota(jnp.int32, (CHUNK, 2 * DN_DIM), 1)
    for p in range(DN_HEADS // 2 if "qk4" not in shared else 0):
        sl = slice(2 * p * DN_DIM, (2 * p + 2) * DN_DIM)
        kp = k[:, sl].astype(BF16)
        zero = jnp.zeros_like(kp)
        rhs_t = jnp.concatenate([jnp.where(lane256 < DN_DIM, kp, zero), jnp.where(lane256 < DN_DIM, zero, kp)],
                                axis=0)
        lhs = jnp.concatenate([q[:, sl].astype(BF16), kp], axis=0)
        prod = lax.dot_general(lhs, rhs_t, (((1,), (1,)), ((), ())), preferred_element_type=F32)
        qk_parts.append(prod[:CHUNK])
        kk_parts.append(prod[CHUNK:])
    if qk_parts:
        shared["qk4"] = jnp.concatenate(qk_parts, axis=-1)
        shared["kk4"] = jnp.concatenate(kk_parts, axis=-1)
    qk4, kk4 = shared["qk4"], shared["kk4"]
    yield
    lane_sel = lax.broadcasted_iota(jnp.int32, (LANES, QUAD), 0)
    quad_head = lax.broadcasted_iota(jnp.int32, (LANES, QUAD), 1) // CHUNK
    sel_g = jnp.where(lane_sel == GATE_G + col0 + quad_head, 1.0, 0.0).astype(BF16)
    sel_b = jnp.where(lane_sel == GATE_B + col0 + quad_head, 1.0, 0.0).astype(BF16)
    colmat = _hilo_dot(gc, sel_g)
    beta4 = _hilo_dot(gate, sel_b)
    yield
    rowmat = _dot_hilo(jnp.ones((CHUNK, CHUNK), BF16), jnp.where(eye4, colmat, 0.0))
    yield
    dstrict = jnp.where(strict, jnp.exp(jnp.where(strict, colmat - rowmat, 0.0)), 0.0)
    dmat = jnp.where(eye4, 1.0, dstrict)
    nmat = -(beta4 * kk4 * dstrict)
    t4 = jnp.where(eye4, 1.0, 0.0) + nmat
    nb16 = nmat.astype(BF16)
    p4 = _dot(nb16, _blockdiag(nb16, DN_HEADS, CHUNK))
    yield
    for _ in range(int(np.log2(CHUNK)) - 2):
        pb16 = p4.astype(BF16)
        both = _dot(jnp.concatenate([t4.astype(BF16), pb16], axis=0), _blockdiag(pb16, DN_HEADS, CHUNK))
        t4 = t4 + both[:CHUNK]
        p4 = both[CHUNK:]
        yield
    t4 = t4 + _dot(t4.astype(BF16), _blockdiag(p4.astype(BF16), DN_HEADS, CHUNK))
    yield

    def wide_cols(tile, base):
        return jnp.concatenate(
            [jnp.broadcast_to(tile[:, base + col0 + h:base + col0 + h + 1], (CHUNK, DN_DIM))
             for h in range(DN_HEADS)], axis=-1)

    beta_w = wide_cols(gate, GATE_B)
    e_w = wide_cols(etile, GATE_G)
    vb = v * beta_w
    kbg = k * beta_w * e_w
    q_dec = (q * e_w).astype(BF16)
    k_dec = k * wide_cols(ektile, GATE_G)
    cd_row = wide_cols(cdtile, GATE_G)[0:1, :]
    rhs = jnp.concatenate([vb, kbg], axis=-1).astype(BF16)
    sol = _dot(t4.astype(BF16), _blockdiag(rhs, DN_HEADS, DN_DIM))
    u = sol[:, :DN_WIDTH]
    w = sol[:, DN_WIDTH:].astype(BF16)
    qk_out = (qk4 * dmat).astype(BF16)
    kd_t = []
    for p in range(DN_HEADS // 2):
        pair = jnp.concatenate([k_dec[:, 2 * p * DN_DIM:(2 * p + 1) * DN_DIM],
                                k_dec[:, (2 * p + 1) * DN_DIM:(2 * p + 2) * DN_DIM]], axis=0)
        kd_t.append(pair.T)
    kd_t = jnp.concatenate(kd_t, axis=-1).astype(BF16)
    return u, w, q_dec, qk_out, kd_t, cd_row


def _dn_prep_kernel(cur_ref, prev_ref, next_ref, ba_ref, meta_ref, cw_ref, gpar_ref,
                    u_ref, w_ref, qd_ref, qk_ref, kdt_ref, cd_ref,
                    ext_s, q_s, k_s, v_s, gate_s, gc_s, e_s, ek_s, cdt_s):
    t = pl.program_id(1)
    n = DN_TILE
    nqkv = 3 * DN_WIDTH
    prev = jnp.where(t == 0, meta_ref[N_META - HALO:N_META, :nqkv], prev_ref[0])
    nxt = jnp.where(t == pl.num_programs(1) - 1, jnp.zeros((HALO, nqkv), F32), next_ref[0])
    ext_s[0:HALO, :] = prev
    ext_s[HALO:HALO + n, :] = cur_ref[0]
    ext_s[HALO + n:2 * HALO + n, :] = nxt
    q, k, v = _normalize_heads(_conv_act(ext_s, cw_ref[...], n))
    q_s[...] = q
    k_s[...] = k
    v_s[...] = v
    gate = _gate_tile(ba_ref[0], gpar_ref[...])
    gate_s[...] = gate
    gc_s[...], e_s[...], ek_s[...], cdt_s[...] = _decay_tiles(gate, n)

    units = [(c, d) for c in range(n // CHUNK) for d in range(2)]
    gens = []
    shared = [{} for _ in range(n // CHUNK)]
    for c, d in units:
        rows = slice(c * CHUNK, (c + 1) * CHUNK)
        gens.append(_chunk_wy(q_s[rows, :], k_s[rows, :], v_s[rows, :],
                              gate_s[rows, :], gc_s[rows, :], e_s[rows, :], ek_s[rows, :], cdt_s[rows, :],
                              d, shared[c]))
    for (c, d), (u, w, qd, qk, kdt, cd) in zip(units, _lockstep(gens)):
        rows = slice(c * CHUNK, (c + 1) * CHUNK)
        u_ref[d, 0, rows, :] = u
        w_ref[d, 0, rows, :] = w
        qd_ref[d, 0, rows, :] = qd
        qk_ref[d, 0, rows, :] = qk
        kdt_ref[d, 0, 2 * c * CHUNK:2 * (c + 1) * CHUNK, :] = kdt
        cd_ref[d, 0, c, 0, :, :] = cd


def _dn_prep(dn_x, ba_x, dn_m, cw, gpar, B, S):
    n = DN_TILE
    nch = n // CHUNK
    nqkv = 3 * DN_WIDTH
    dn3 = dn_x.reshape(B, S, dn_x.shape[-1])
    ba3 = ba_x.reshape(B, S, LANES)
    hb = n // HALO
    nhb = S // HALO
    outs = [
        ((2, B, S, DN_WIDTH), F32, (2, 1, n, DN_WIDTH)),
        ((2, B, S, DN_WIDTH), BF16, (2, 1, n, DN_WIDTH)),
        ((2, B, S, DN_WIDTH), BF16, (2, 1, n, DN_WIDTH)),
        ((2, B, S, QUAD), BF16, (2, 1, n, QUAD)),
        ((2, B, 2 * S, QUAD), BF16, (2, 1, 2 * n, QUAD)),
    ]
    out_specs = [pl.BlockSpec(blk, lambda b, t: (0, b, t, 0)) for _, _, blk in outs]
    out_shape = [jax.ShapeDtypeStruct(shp, dt) for shp, dt, _ in outs]
    out_specs.append(pl.BlockSpec((2, 1, nch, 1, 1, DN_WIDTH), lambda b, t: (0, b, t, 0, 0, 0)))
    out_shape.append(jax.ShapeDtypeStruct((2, B, S // CHUNK, 1, 1, DN_WIDTH), F32))
    return pl.pallas_call(
        _dn_prep_kernel,
        grid=(B, S // n),
        in_specs=[
            pl.BlockSpec((1, n, nqkv), lambda b, t: (b, t, 0)),
            pl.BlockSpec((1, HALO, nqkv), lambda b, t: (b, jnp.maximum(t * hb - 1, 0), 0)),
            pl.BlockSpec((1, HALO, nqkv), lambda b, t: (b, jnp.minimum((t + 1) * hb, nhb - 1), 0)),
            pl.BlockSpec((1, n, LANES), lambda b, t: (b, t, 0)),
            _const_spec(dn_m.shape), _const_spec(cw.shape), _const_spec(gpar.shape),
        ],
        out_specs=out_specs,
        out_shape=out_shape,
        scratch_shapes=[pltpu.VMEM((n + 2 * HALO, nqkv), F32)] + [pltpu.VMEM((n, DN_WIDTH), F32)] * 3
        + [pltpu.VMEM((n, LANES), F32)] * 5,
        compiler_params=_params("parallel", "parallel"),
        name="dn_prep",
    )(dn3, dn3, dn3, ba3, dn_m, cw, gpar)


def _state_step(s_ref, u, w, qd, qk, kdt, cd):
    s = s_ref[...]
    ws_parts, qs_parts = [], []
    for p in range(DN_HEADS // 2):
        sl = slice(2 * p * DN_DIM, (2 * p + 2) * DN_DIM)
        bd = _blockdiag(s[:, sl].astype(BF16), 2, DN_DIM)
        both = _dot(jnp.concatenate([w[:, sl], qd[:, sl]], axis=0), bd)
        ws_parts.append(both[:CHUNK])
        qs_parts.append(both[CHUNK:])
    yield
    v_new = u - jnp.concatenate(ws_parts, axis=-1)
    both = _dot(jnp.concatenate([qk, kdt], axis=0), _blockdiag(v_new.astype(BF16), DN_HEADS, DN_DIM))
    yield
    s_ref[...] = s * cd + both[CHUNK:]
    return jnp.concatenate(qs_parts, axis=-1) + both[:CHUNK]


def _dn_scan_kernel(sinit_ref, uf, wf, qdf, qkf, kdtf, cdf, ub, wb, qdb, qkb, kdtb, cdb,
                    of_ref, ob_ref, sf, sb, *, nch, nbatch):
    @pl.when(pl.program_id(1) == 0)
    def _():
        sf[...] = sinit_ref[...]
        sb[...] = jnp.zeros_like(sb)

    def body(c, carry):
        cb = nch - 1 - c
        rf = pl.ds(pl.multiple_of(c * CHUNK, CHUNK), CHUNK)
        rb = pl.ds(pl.multiple_of(cb * CHUNK, CHUNK), CHUNK)
        rf2 = pl.ds(pl.multiple_of(c * 2 * CHUNK, 2 * CHUNK), 2 * CHUNK)
        rb2 = pl.ds(pl.multiple_of(cb * 2 * CHUNK, 2 * CHUNK), 2 * CHUNK)
        gens = []
        for e in range(nbatch):
            gens.append(_state_step(sf.at[e], uf[0, e, rf, :], wf[0, e, rf, :], qdf[0, e, rf, :], qkf[0, e, rf, :],
                                    kdtf[0, e, rf2, :], cdf[0, e, c, 0]))
            gens.append(_state_step(sb.at[e], ub[0, e, rb, :], wb[0, e, rb, :], qdb[0, e, rb, :], qkb[0, e, rb, :],
                                    kdtb[0, e, rb2, :], cdb[0, e, cb, 0]))
        outs = _lockstep(gens)
        for e in range(nbatch):
            of_ref[e, rf, :] = outs[2 * e]
            ob_ref[e, rb, :] = outs[2 * e + 1]
        return carry

    lax.fori_loop(0, nch, body, 0)


def _dn_scan(sinit, u, w, qd, qk, kdt, cd, B, S):
    nch = DN_SCAN_CHUNKS
    n = nch * CHUNK
    nb = S // n
    nbatch = DN_SCAN_BATCH if B % DN_SCAN_BATCH == 0 else 1

    def specs(d):
        im = (lambda b, i: (d, b, i, 0)) if d == 0 else (lambda b, i: (d, b, nb - 1 - i, 0))
        im6 = (lambda b, i: (d, b, i, 0, 0, 0)) if d == 0 else (lambda b, i: (d, b, nb - 1 - i, 0, 0, 0))
        return [pl.BlockSpec((1, nbatch, n, DN_WIDTH), im), pl.BlockSpec((1, nbatch, n, DN_WIDTH), im),
                pl.BlockSpec((1, nbatch, n, DN_WIDTH), im), pl.BlockSpec((1, nbatch, n, QUAD), im),
                pl.BlockSpec((1, nbatch, 2 * n, QUAD), im), pl.BlockSpec((1, nbatch, nch, 1, 1, DN_WIDTH), im6)]

    return pl.pallas_call(
        functools.partial(_dn_scan_kernel, nch=nch, nbatch=nbatch),
        grid=(B // nbatch, nb),
        in_specs=[pl.BlockSpec((nbatch, DN_DIM, DN_WIDTH), lambda b, i: (b, 0, 0))] + specs(0) + specs(1),
        out_specs=[pl.BlockSpec((nbatch, n, DN_WIDTH), lambda b, i: (b, i, 0)),
                   pl.BlockSpec((nbatch, n, DN_WIDTH), lambda b, i: (b, nb - 1 - i, 0))],
        out_shape=[jax.ShapeDtypeStruct((B, S, DN_WIDTH), F32)] * 2,
        scratch_shapes=[pltpu.VMEM((nbatch, DN_DIM, DN_WIDTH), F32)] * 2,
        compiler_params=_params("parallel", "arbitrary"),
        name="dn_scan",
    )(sinit, u, w, qd, qk, kdt, cd, u, w, qd, qk, kdt, cd)


def _dn_meta_kernel(x0_ref, meta_ref, bam_ref, cw_ref, gpar_ref, sinit_ref, ext_s):
    nqkv = 3 * DN_WIDTH
    ext_s[0:HALO, :] = jnp.zeros((HALO, nqkv), F32)
    ext_s[HALO:HALO + N_META, :] = meta_ref[:, :nqkv]
    ext_s[HALO + N_META:, :] = x0_ref[0]
    q, k, v = _normalize_heads(_conv_act(ext_s, cw_ref[...], N_META))
    pad = CHUNK - N_META
    zw = jnp.zeros((pad, DN_WIDTH), F32)
    q, k, v = (jnp.concatenate([zw, a], axis=0) for a in (q, k, v))
    gate = jnp.concatenate([jnp.zeros((pad, LANES), F32), _gate_tile(bam_ref[...], gpar_ref[...])], axis=0)
    gc, e, ek, cdt = _decay_tiles(gate, CHUNK)
    (u, _, _, _, kdt, _), = _lockstep([_chunk_wy(q, k, v, gate, gc, e, ek, cdt, 0, {})])
    sinit_ref[0] = _dot(kdt, _blockdiag(u.astype(BF16), DN_HEADS, DN_DIM))


def _dn_meta(dn_x, dn_m, ba_m, cw, gpar, B, S):
    nqkv = 3 * DN_WIDTH
    dn3 = dn_x.reshape(B, S, dn_x.shape[-1])
    return pl.pallas_call(
        _dn_meta_kernel,
        grid=(B,),
        in_specs=[pl.BlockSpec((1, HALO, nqkv), lambda b: (b, 0, 0)),
                  _const_spec(dn_m.shape), _const_spec(ba_m.shape), _const_spec(cw.shape), _const_spec(gpar.shape)],
        out_specs=pl.BlockSpec((1, DN_DIM, DN_WIDTH), lambda b: (b, 0, 0)),
        out_shape=jax.ShapeDtypeStruct((B, DN_DIM, DN_WIDTH), F32),
        scratch_shapes=[pltpu.VMEM((2 * HALO + N_META, nqkv), F32)],
        compiler_params=_params("parallel"),
        name="dn_meta",
    )(dn3, dn_m, ba_m, cw, gpar)


def _gate_params(a_log, dt_bias):
    neg_a = jnp.zeros((LANES,), F32).at[GATE_G:GATE_G + 2 * DN_HEADS].set(-jnp.exp(a_log.astype(F32)).reshape(-1))
    dtb = jnp.zeros((LANES,), F32).at[GATE_G:GATE_G + 2 * DN_HEADS].set(dt_bias.astype(F32).reshape(-1))
    return jnp.concatenate([neg_a[None], dtb[None], jnp.zeros((6, LANES), F32)], axis=0)


def _deltanet(dn_x, ba_x, dn_m, ba_m, conv_w, a_log, dt_bias, B, S):
    gpar = _gate_params(a_log, dt_bias)
    cw = jnp.pad(conv_w.astype(F32), ((0, 8 - CONV_W), (0, 0)))
    u, w, qd, qk, kdt, cd = _dn_prep(dn_x, ba_x, dn_m, cw, gpar, B, S)
    sinit = _dn_meta(dn_x, dn_m, ba_m, cw, gpar, B, S)
    return _dn_scan(sinit, u, w, qd, qk, kdt, cd, B, S)


INPROJ_TM = 1024


def kernel(x, meta_tokens, g_mix, w_in, na_rel_bias, dn_conv_w, dn_a_log, dn_dt_bias, dn_norm_g,
           w_out, g_ffn, w_gate, w_up, w_down, g_final):
    B, S, D = x.shape
    assert w_in.shape[0] == 1 and S % (NA_ROWS_PER_STEP * GRID_W) == 0 and S % DN_TILE == 0
    l = 0
    nqkv = 3 * NA_WIDTH
    ndn = 4 * DN_WIDTH
    w = w_in[l]
    wna = jnp.concatenate([w[:, :NA_WIDTH] * (NA_HEAD_DIM ** -0.5), w[:, NA_WIDTH:nqkv]], axis=1).astype(BF16)
    wdn = w[:, nqkv:nqkv + ndn].astype(BF16)
    wba = jnp.pad(w[:, nqkv + ndn:], ((0, 0), (0, LANES - 4 * DN_HEADS))).astype(BF16)
    gm = g_mix[l][None, :]

    x2d = x.reshape(B * S, D)
    na_x, dn_x, ba_x = _inproj(x2d, gm, wna, wdn, wba, tm=INPROJ_TM)
    na_m, dn_m, ba_m = _inproj(meta_tokens, gm, wna, wdn, wba, tm=N_META)

    y_na = _na(na_x, na_m, _na_bias_table(na_rel_bias[l]), B, S)
    o_f, o_b = _deltanet(dn_x, ba_x, dn_m, ba_m, dn_conv_w[l], dn_a_log[l], dn_dt_bias[l], B, S)

    wo = w_out[l].astype(BF16)
    out = _tail(x2d, y_na.reshape(B * S, NA_WIDTH), o_f.reshape(B * S, DN_WIDTH), o_b.reshape(B * S, DN_WIDTH),
                dn_x, dn_norm_g[l][None, :], wo[:NA_WIDTH], wo[NA_WIDTH:], g_ffn[l][None, :],
                w_gate[l].astype(BF16), w_up[l].astype(BF16), w_down[l].astype(BF16),
                g_final[None, :], tm=512, ff_chunk=256)
    return out.reshape(B, S, D)
```
